```python
import jax, jax.numpy as jnp
from jax import lax
import numpy as np

D_MODEL = 1024
BATCH = 4
SEQ = 8192
DEPTH = 2

GRID_W = 64
CTX_LEN = 256
RMS_EPS = 1e-6
N_BRANCH = 3

F_WIDTH = D_MODEL // 4
F_GROUPS = 4
F_GROUP_DIM = F_WIDTH // F_GROUPS

S5_WIDTH = D_MODEL // 4
S5_GROUP_DIM = 16
S5_GROUPS = S5_WIDTH // S5_GROUP_DIM
S5_STATE = 64
S5_DT_MIN = 1e-3
S5_DT_MAX = 1e-1

NA_WIDTH = D_MODEL // 2
NA_HEAD_DIM = 64
NA_HEADS = NA_WIDTH // NA_HEAD_DIM
NA_KH = 8
NA_KW = 16

D_FF = 4 * D_MODEL

OFF_S5 = F_WIDTH
OFF_Q = OFF_S5 + S5_WIDTH
OFF_K = OFF_Q + NA_WIDTH
OFF_V = OFF_K + NA_WIDTH
OFF_G = OFF_V + NA_WIDTH
IN_WIDTH = OFF_G + N_BRANCH * D_MODEL

kernel_name = "hybrid_fnet_s5_natten_diffusion_block"


def rmsnorm(x, g):
    xf = x.astype(jnp.float32)
    y = xf * lax.rsqrt(jnp.mean(xf * xf, axis=-1, keepdims=True) + RMS_EPS)
    return (y * g.astype(jnp.float32)).astype(x.dtype)


def modulate(h, shift, scale):
    return h * (1 + scale) + shift


def sqrelu_mlp(h, w1, w2):
    return jnp.square(jax.nn.relu(h @ w1)) @ w2


def fourier_mix(z):
    b, l, _ = z.shape
    zg = z.astype(jnp.float32).reshape(b, l, F_GROUPS, F_GROUP_DIM)
    y = jnp.fft.fftn(zg, axes=(1, 3), norm="ortho").real
    return y.reshape(b, l, F_WIDTH).astype(z.dtype)


def _scan_op(e1, e2):
    a1, b1 = e1
    a2, b2 = e2
    return a1 * a2, a2 * b1 + b2


def s5_discretise(lam_re, lam_im, log_dt, b_re, b_im):
    lam = lax.complex(jnp.minimum(lam_re.astype(jnp.float32), -1e-4), lam_im.astype(jnp.float32))
    dt = jnp.exp(log_dt.astype(jnp.float32))[:, None]
    lbar = jnp.exp(lam * dt)
    bmat = lax.complex(b_re.astype(jnp.float32), b_im.astype(jnp.float32))
    bbar = ((lbar - 1) / lam)[..., None] * bmat
    return lbar, bbar


def s5_drive(u, bbar):
    return jnp.einsum('blgn,gpn->blgp', u.astype(jnp.complex64), bbar)


def linear_scan(bu, lbar, h0, reverse):
    if h0 is not None:
        idx = -1 if reverse else 0
        bu = bu.at[:, idx].add(lbar * h0)
    a = jnp.broadcast_to(lbar, bu.shape)
    _, h = lax.associative_scan(_scan_op, (a, bu), axis=1, reverse=reverse)
    return h


def s5_readout(h, cmat):
    return jnp.einsum('blgp,gnp->blgn', h, cmat).real


def s5_glu(y, w_glu):
    g = jax.nn.gelu(y)
    return g * jax.nn.sigmoid(g @ w_glu.astype(jnp.float32))


def s5_branch(zx, zc, lam_re, lam_im, log_dt, b_re, b_im, c_re, c_im, d_skip, w_glu, need_ctx_out):
    b, l, _ = zx.shape
    lc = zc.shape[1]
    ux = zx.astype(jnp.float32).reshape(b, l, S5_GROUPS, S5_GROUP_DIM)
    uc = zc.astype(jnp.float32).reshape(b, lc, S5_GROUPS, S5_GROUP_DIM)
    d = d_skip.astype(jnp.float32)
    yx = ux * d
    yc = uc * d if need_ctx_out else None
    for direction, reverse in ((0, False), (1, True)):
        lbar, bbar = s5_discretise(lam_re[direction], lam_im[direction], log_dt[direction],
                                   b_re[direction], b_im[direction])
        cmat = lax.complex(c_re[direction].astype(jnp.float32), c_im[direction].astype(jnp.float32))
        hc = linear_scan(s5_drive(uc, bbar), lbar, None, reverse)
        hc_last = hc[:, 0] if reverse else hc[:, -1]
        hx = linear_scan(s5_drive(ux, bbar), lbar, hc_last, reverse)
        yx = yx + s5_readout(hx, cmat)
        if need_ctx_out:
            yc = yc + s5_readout(hc, cmat)
    out_x = s5_glu(yx.reshape(b, l, S5_WIDTH), w_glu).astype(zx.dtype)
    out_c = s5_glu(yc.reshape(b, lc, S5_WIDTH), w_glu).astype(zc.dtype) if need_ctx_out else None
    return out_x, out_c


def na_latent(q, k, v, kc, vc, rpb):
    b, l, h, dh = q.shape
    rows = l // GRID_W
    kh = min(NA_KH, rows)
    qg = q.reshape(b, rows, GRID_W, h, dh)
    kg = k.reshape(b, rows, GRID_W, h, dh)
    vg = v.reshape(b, rows, GRID_W, h, dh)
    col = np.arange(GRID_W)
    col_start = np.clip(col - NA_KW // 2, 0, GRID_W - NA_KW)
    col_idx = col_start[:, None] + np.arange(NA_KW)[None, :]
    col_rel = col_idx - col[:, None] + NA_KW - 1
    rpb_cols = rpb.astype(jnp.float32)[:, :, col_rel]
    scale = dh ** -0.5

    def one_row(r):
        rs = jnp.clip(r - kh // 2, 0, rows - kh)
        k_band = lax.dynamic_slice_in_dim(kg, rs, kh, axis=1)
        v_band = lax.dynamic_slice_in_dim(vg, rs, kh, axis=1)
        k_win = k_band[:, :, col_idx]
        v_win = v_band[:, :, col_idx]
        q_row = lax.dynamic_index_in_dim(qg, r, axis=1, keepdims=False)
        row_rel = rs + jnp.arange(kh) - r + NA_KH - 1
        bias = jnp.take(rpb_cols, row_rel, axis=1).transpose(0, 2, 1, 3)
        s_loc = jnp.einsum('bwhd,bkwjhd->bhwkj', q_row, k_win).astype(jnp.float32) * scale + bias[None]
        s_ctx = jnp.einsum('bwhd,bmhd->bhwm', q_row, kc).astype(jnp.float32) * scale
        s = jnp.concatenate([s_loc.reshape(b, h, GRID_W, kh * NA_KW), s_ctx], axis=-1)
        p = jax.nn.softmax(s, axis=-1).astype(v.dtype)
        p_loc = p[..., :kh * NA_KW].reshape(b, h, GRID_W, kh, NA_KW)
        p_ctx = p[..., kh * NA_KW:]
        return (jnp.einsum('bhwkj,bkwjhd->bwhd', p_loc, v_win)
                + jnp.einsum('bhwm,bmhd->bwhd', p_ctx, vc))

    out = lax.map(one_row, jnp.arange(rows))
    return jnp.moveaxis(out, 0, 1).reshape(b, l, h * dh)


def ctx_attention(qc, kc, vc):
    b, lc, h, dh = qc.shape
    s = jnp.einsum('bmhd,bnhd->bhmn', qc, kc).astype(jnp.float32) * dh ** -0.5
    p = jax.nn.softmax(s, axis=-1).astype(vc.dtype)
    return jnp.einsum('bhmn,bnhd->bmhd', p, vc).reshape(b, lc, h * dh)


def gated_merge(fa, sb, nc, gate_logits, w_br_a, w_br_b, w_br_c, w_out):
    g = jax.nn.sigmoid(gate_logits)
    m = (g[..., :D_MODEL] * (fa @ w_br_a)
         + g[..., D_MODEL:2 * D_MODEL] * (sb @ w_br_b)
         + g[..., 2 * D_MODEL:] * (nc @ w_br_c))
    return m @ w_out


def mixer_sublayer(hx, hc, w_in, w_br_a, w_br_b, w_br_c, w_out, lam_re, lam_im, log_dt,
                   b_re, b_im, c_re, c_im, d_skip, w_glu, rpb, need_ctx_out):
    b, l, _ = hx.shape
    lc = hc.shape[1]
    heads = (NA_HEADS, NA_HEAD_DIM)
    zx = hx @ w_in
    zc_s5 = hc @ w_in[:, OFF_S5:OFF_Q]
    zc_kv = hc @ w_in[:, OFF_K:OFF_G]
    kc = zc_kv[..., :NA_WIDTH].reshape(b, lc, *heads)
    vc = zc_kv[..., NA_WIDTH:].reshape(b, lc, *heads)

    fa_x = fourier_mix(zx[..., :OFF_S5])
    sb_x, sb_c = s5_branch(zx[..., OFF_S5:OFF_Q], zc_s5, lam_re, lam_im, log_dt,
                           b_re, b_im, c_re, c_im, d_skip, w_glu, need_ctx_out)
    qx = zx[..., OFF_Q:OFF_K].reshape(b, l, *heads)
    kx = zx[..., OFF_K:OFF_V].reshape(b, l, *heads)
    vx = zx[..., OFF_V:OFF_G].reshape(b, l, *heads)
    nc_x = na_latent(qx, kx, vx, kc, vc, rpb)
    out_x = gated_merge(fa_x, sb_x, nc_x, zx[..., OFF_G:], w_br_a, w_br_b, w_br_c, w_out)

    out_c = None
    if need_ctx_out:
        fa_c = fourier_mix(hc @ w_in[:, :OFF_S5])
        qc = (hc @ w_in[:, OFF_Q:OFF_K]).reshape(b, lc, *heads)
        nc_c = ctx_attention(qc, kc, vc)
        out_c = gated_merge(fa_c, sb_c, nc_c, hc @ w_in[:, OFF_G:], w_br_a, w_br_b, w_br_c, w_out)
    return out_x, out_c


def setup_inputs(seed: int = 0) -> dict:
    key = jax.random.key(seed)
    ks = iter(jax.random.split(key, 32))
    f32 = jnp.float32

    def nrm(shape, scale):
        return jax.random.normal(next(ks), shape, f32) * scale

    L, G, P, N = DEPTH, S5_GROUPS, S5_STATE, S5_GROUP_DIM
    x = nrm((BATCH, SEQ, D_MODEL), 1.0)
    c = nrm((BATCH, D_MODEL), 1.0)
    ctx = nrm((BATCH, CTX_LEN, D_MODEL), 1.0)
    c_ctx = nrm((D_MODEL,), 1.0)
    w_mod = nrm((L, D_MODEL, 6 * D_MODEL), D_MODEL ** -0.5)
    b_mod = nrm((L, 6 * D_MODEL), 0.01)
    g_norm1 = 1.0 + nrm((L, D_MODEL), 0.02)
    g_norm2 = 1.0 + nrm((L, D_MODEL), 0.02)
    w_in = nrm((L, D_MODEL, IN_WIDTH), D_MODEL ** -0.5)
    w_br_a = nrm((L, F_WIDTH, D_MODEL), F_WIDTH ** -0.5)
    w_br_b = nrm((L, S5_WIDTH, D_MODEL), S5_WIDTH ** -0.5)
    w_br_c = nrm((L, NA_WIDTH, D_MODEL), NA_WIDTH ** -0.5)
    w_out = nrm((L, D_MODEL, D_MODEL), D_MODEL ** -0.5)
    s5_lam_re = -0.5 + nrm((L, 2, G, P), 0.01)
    s5_lam_im = jnp.pi * jnp.arange(P, dtype=f32) + nrm((L, 2, G, P), 0.01)
    s5_log_dt = jax.random.uniform(next(ks), (L, 2, G), f32,
                                   minval=math_log(S5_DT_MIN), maxval=math_log(S5_DT_MAX))
    s5_b_re = nrm((L, 2, G, P, N), (2 * N) ** -0.5)
    s5_b_im = nrm((L, 2, G, P, N), (2 * N) ** -0.5)
    s5_c_re = nrm((L, 2, G, N, P), (2 * P) ** -0.5)
    s5_c_im = nrm((L, 2, G, N, P), (2 * P) ** -0.5)
    s5_d = nrm((L, G, N), 1.0)
    s5_w_glu = nrm((L, S5_WIDTH, S5_WIDTH), S5_WIDTH ** -0.5)
    na_rpb = nrm((L, NA_HEADS, 2 * NA_KH - 1, 2 * NA_KW - 1), 0.02)
    w_ff1 = nrm((L, D_MODEL, D_FF), D_MODEL ** -0.5)
    w_ff2 = nrm((L, D_FF, D_MODEL), D_FF ** -0.5)
    g_final = 1.0 + nrm((D_MODEL,), 0.02)
    return {"x": x, "c": c, "ctx": ctx, "c_ctx": c_ctx, "w_mod": w_mod, "b_mod": b_mod,
            "g_norm1": g_norm1, "g_norm2": g_norm2, "w_in": w_in, "w_br_a": w_br_a,
            "w_br_b": w_br_b, "w_br_c": w_br_c, "w_out": w_out, "s5_lam_re": s5_lam_re,
            "s5_lam_im": s5_lam_im, "s5_log_dt": s5_log_dt, "s5_b_re": s5_b_re, "s5_b_im": s5_b_im,
            "s5_c_re": s5_c_re, "s5_c_im": s5_c_im, "s5_d": s5_d, "s5_w_glu": s5_w_glu,
            "na_rpb": na_rpb, "w_ff1": w_ff1, "w_ff2": w_ff2, "g_final": g_final}


def math_log(v):
    return float(np.log(v))


def reference(x, c, ctx, c_ctx, w_mod, b_mod, g_norm1, g_norm2, w_in, w_br_a, w_br_b, w_br_c, w_out,
              s5_lam_re, s5_lam_im, s5_log_dt, s5_b_re, s5_b_im, s5_c_re, s5_c_im, s5_d, s5_w_glu,
              na_rpb, w_ff1, w_ff2, g_final):
    for i in range(DEPTH):
        need_ctx_out = i < DEPTH - 1
        mod_x = jax.nn.silu(c) @ w_mod[i] + b_mod[i]
        sh1, sc1, gt1, sh2, sc2, gt2 = jnp.split(mod_x[:, None, :], 6, axis=-1)
        n_ctx_mod = 6 if need_ctx_out else 2
        mod_c = jax.nn.silu(c_ctx) @ w_mod[i][:, :n_ctx_mod * D_MODEL] + b_mod[i][:n_ctx_mod * D_MODEL]
        mod_c = jnp.split(mod_c, n_ctx_mod)

        hx = modulate(rmsnorm(x, g_norm1[i]), sh1, sc1)
        hc = modulate(rmsnorm(ctx, g_norm1[i]), mod_c[0], mod_c[1])
        ox, oc = mixer_sublayer(hx, hc, w_in[i], w_br_a[i], w_br_b[i], w_br_c[i], w_out[i],
                                s5_lam_re[i], s5_lam_im[i], s5_log_dt[i], s5_b_re[i], s5_b_im[i],
                                s5_c_re[i], s5_c_im[i], s5_d[i], s5_w_glu[i], na_rpb[i], need_ctx_out)
        x = x + gt1 * ox
        h2 = modulate(rmsnorm(x, g_norm2[i]), sh2, sc2)
        x = x + gt2 * sqrelu_mlp(h2, w_ff1[i], w_ff2[i])

        if need_ctx_out:
            ctx = ctx + mod_c[2] * oc
            h2c = modulate(rmsnorm(ctx, g_norm2[i]), mod_c[3], mod_c[4])
            ctx = ctx + mod_c[5] * sqrelu_mlp(h2c, w_ff1[i], w_ff2[i])
    return rmsnorm(x, g_final)
```

```python
import functools
import math

import numpy as np
import jax
import jax.numpy as jnp
from jax import lax
from jax.experimental import pallas as pl
from jax.experimental.pallas import tpu as pltpu

_F32 = jnp.float32
_BF16 = jnp.bfloat16

RMS_EPS = 1e-6
GRID_W = 64
F_GROUP_DIM = 64
S5_GROUP_DIM = 16
NA_HEAD_DIM = 64
NA_KH = 8
NA_KW = 16
NA_QROWS = 8
NEG_BIG = -1e30
FFT_L2 = 128
S5_TS = 64

VMEM_LIMIT_BYTES = 56 * 1024 * 1024


def _cparams(*sem):
    return pltpu.CompilerParams(dimension_semantics=sem, vmem_limit_bytes=VMEM_LIMIT_BYTES)


def _dot(a, b):
    return jnp.dot(a, b, preferred_element_type=_F32)


def _table_bf16(t):
    return jnp.asarray(t, _F32).astype(_BF16)


def _dot_nt(a, b):
    return lax.dot_general(a, b, (((1,), (1,)), ((), ())), preferred_element_type=_F32)


def _norm_mod(x, g, shift, scale):
    ms = jnp.mean(x * x, axis=-1, keepdims=True)
    y = x * lax.rsqrt(ms + RMS_EPS) * g
    return y * (1.0 + scale) + shift


def _mod_kernel(c_ref, w_ref, b_ref, o_ref):
    c = c_ref[...]
    s = c * jax.nn.sigmoid(c)
    o_ref[...] = jnp.dot(s, w_ref[...], precision=lax.Precision.HIGHEST,
                         preferred_element_type=_F32) + b_ref[...]


def _modulation(c8, w_mod, b_mod):
    depth, d, n = w_mod.shape
    tn = 1024
    return pl.pallas_call(
        _mod_kernel,
        grid=(depth, n // tn),
        in_specs=[pl.BlockSpec((8, d), lambda l, j: (0, 0)),
                  pl.BlockSpec((None, d, tn), lambda l, j: (l, 0, j)),
                  pl.BlockSpec((None, 1, tn), lambda l, j: (l, 0, j))],
        out_specs=pl.BlockSpec((None, 8, tn), lambda l, j: (l, 0, j)),
        out_shape=jax.ShapeDtypeStruct((depth, 8, n), _F32),
        compiler_params=_cparams("parallel", "parallel"),
        name="modulation",
    )(c8, w_mod, b_mod.reshape(depth, 1, n))


def _inproj_kernel(x_ref, mod_ref, g_ref, w_ref, zf_ref, zs_ref, q_ref, k_ref, v_ref, *, widths, q_scale):
    h = _norm_mod(x_ref[...], g_ref[...], mod_ref[0:1, :], mod_ref[1:2, :]).astype(_BF16)
    fw, sw, aw = widths
    o = 0
    zf_ref[...] = _dot(h, w_ref[:, o:o + fw]).astype(_BF16)
    o += fw
    zs_ref[...] = _dot(h, w_ref[:, o:o + sw]).astype(_BF16)
    o += sw
    q_ref[...] = (_dot(h, w_ref[:, o:o + aw]) * q_scale).astype(_BF16)
    o += aw
    k_ref[...] = _dot(h, w_ref[:, o:o + aw]).astype(_BF16)
    o += aw
    v_ref[...] = _dot(h, w_ref[:, o:o + aw]).astype(_BF16)


def _inproj(x, mod, g, w, widths, tm):
    b, l, d = x.shape
    fw, sw, aw = widths
    n = w.shape[1]
    tok = lambda width: pl.BlockSpec((None, tm, width), lambda bi, i: (bi, i, 0))
    return pl.pallas_call(
        functools.partial(_inproj_kernel, widths=widths, q_scale=NA_HEAD_DIM ** -0.5),
        grid=(b, l // tm),
        in_specs=[tok(d),
                  pl.BlockSpec((None, 6, d), lambda bi, i: (bi, 0, 0)),
                  pl.BlockSpec((1, d), lambda bi, i: (0, 0)),
                  pl.BlockSpec((d, n), lambda bi, i: (0, 0))],
        out_specs=[tok(fw), pl.BlockSpec((tm, sw), lambda bi, i: (i, bi)), tok(aw), tok(aw), tok(aw)],
        out_shape=[jax.ShapeDtypeStruct((b, l, fw), _BF16), jax.ShapeDtypeStruct((l, b * sw), _BF16),
                   jax.ShapeDtypeStruct((b, l, aw), _BF16), jax.ShapeDtypeStruct((b, l, aw), _BF16),
                   jax.ShapeDtypeStruct((b, l, aw), _BF16)],
        compiler_params=_cparams("parallel", "parallel"),
        name="inproj",
    )(x, mod, g, w)


def _dft_tables(l):
    l1, l2 = l // FFT_L2, FFT_L2
    a1 = 2 * np.pi * np.outer(np.arange(l1), np.arange(l1)) / l1
    f1 = np.concatenate([np.cos(a1), np.sin(a1)], axis=0)
    at = 2 * np.pi * np.outer(np.arange(l1), np.arange(l2)) / l
    a2 = 2 * np.pi * np.outer(np.arange(l2), np.arange(l2)) / l2
    c2, s2 = np.cos(a2), np.sin(a2)
    f2 = np.block([[c2, -s2], [s2, c2]])
    return f1, np.cos(at), np.sin(at), f2


def _group_dft_tables(width):
    ac = 2 * np.pi * np.outer(np.arange(F_GROUP_DIM), np.arange(F_GROUP_DIM)) / F_GROUP_DIM
    eye = np.eye(width // F_GROUP_DIM)
    return np.kron(eye, np.cos(ac)), np.kron(eye, np.sin(ac))


def _fnet_a_kernel(x_ref, f1_ref, cw_ref, sw_ref, o_ref):
    a = _dot(f1_ref[...], x_ref[...])
    l1 = a.shape[0] // 2
    ac, asn = a[:l1], a[l1:]
    cw, sw = cw_ref[...], sw_ref[...]
    o_ref[0] = (ac * cw - asn * sw).astype(_BF16)
    o_ref[1] = (ac * sw + asn * cw).astype(_BF16)


def _fnet_b_kernel(a_ref, f2_ref, bc_ref, bs_ref, o_ref, *, scale, nk1, width):
    half = f2_ref.shape[0] // 2
    for j in range(nk1):
        rhs = jnp.concatenate([a_ref[0, j], a_ref[1, j]], axis=0)
        r = _dot(f2_ref[...], rhs)
        out = _dot(r[:half].astype(_BF16), bc_ref[...]) - _dot(r[half:].astype(_BF16), bs_ref[...])
        o_ref[:, j * width:(j + 1) * width] = (out * scale).astype(_BF16)


def _fnet_latent(zf):
    b, l, width = zf.shape
    l1, l2 = l // FFT_L2, FFT_L2
    f1, cw, sw, f2 = _dft_tables(l)
    bc, bs = _group_dft_tables(width)
    nc = 16 * width
    cw = jnp.repeat(jnp.asarray(cw, _F32), width, axis=1)
    sw = jnp.repeat(jnp.asarray(sw, _F32), width, axis=1)
    a = pl.pallas_call(
        _fnet_a_kernel,
        grid=(l2 * width // nc, b),
        in_specs=[pl.BlockSpec((None, l1, nc), lambda j, bi: (bi, 0, j)),
                  pl.BlockSpec((2 * l1, l1), lambda j, bi: (0, 0)),
                  pl.BlockSpec((l1, nc), lambda j, bi: (0, j)),
                  pl.BlockSpec((l1, nc), lambda j, bi: (0, j))],
        out_specs=pl.BlockSpec((None, 2, l1, nc), lambda j, bi: (bi, 0, 0, j)),
        out_shape=jax.ShapeDtypeStruct((b, 2, l1, l2 * width), _BF16),
        compiler_params=_cparams("parallel", "parallel"),
        name="fnet_a",
    )(zf.reshape(b, l1, l2 * width), _table_bf16(f1), cw, sw)
    nk1 = 8
    out = pl.pallas_call(
        functools.partial(_fnet_b_kernel, scale=float((l * F_GROUP_DIM) ** -0.5), nk1=nk1, width=width),
        grid=(b, l1 // nk1),
        in_specs=[pl.BlockSpec((None, 2, nk1, l2, width), lambda bi, j: (bi, 0, j, 0, 0)),
                  pl.BlockSpec((2 * l2, 2 * l2), lambda bi, j: (0, 0)),
                  pl.BlockSpec((width, width), lambda bi, j: (0, 0)),
                  pl.BlockSpec((width, width), lambda bi, j: (0, 0))],
        out_specs=pl.BlockSpec((None, l2, nk1 * width), lambda bi, j: (bi, 0, j)),
        out_shape=jax.ShapeDtypeStruct((b, l2, l1 * width), _BF16),
        compiler_params=_cparams("parallel", "parallel"),
        name="fnet_b",
    )(a.reshape(b, 2, l1, l2, width), _table_bf16(f2), _table_bf16(bc), _table_bf16(bs))
    return out.reshape(b, l, width)


def _fnet_ctx_kernel(z_ref, cl_ref, sl_ref, bc_ref, bs_ref, o_ref, *, scale):
    z = z_ref[...]
    zc = _dot(z, bc_ref[...]).astype(_BF16)
    zs = _dot(z, bs_ref[...]).astype(_BF16)
    o_ref[...] = ((_dot(cl_ref[...], zc) - _dot(sl_ref[...], zs)) * scale).astype(_BF16)


def _fnet_ctx(zf):
    b, l, width = zf.shape
    al = 2 * np.pi * np.outer(np.arange(l), np.arange(l)) / l
    bc, bs = _group_dft_tables(width)
    full = lambda r, c: pl.BlockSpec((r, c), lambda bi: (0, 0))
    return pl.pallas_call(
        functools.partial(_fnet_ctx_kernel, scale=float((l * F_GROUP_DIM) ** -0.5)),
        grid=(b,),
        in_specs=[pl.BlockSpec((None, l, width), lambda bi: (bi, 0, 0)),
                  full(l, l), full(l, l), full(width, width), full(width, width)],
        out_specs=pl.BlockSpec((None, l, width), lambda bi: (bi, 0, 0)),
        out_shape=jax.ShapeDtypeStruct((b, l, width), _BF16),
        compiler_params=_cparams("parallel"),
        name="fnet_ctx",
    )(zf, _table_bf16(np.cos(al)), _table_bf16(np.sin(al)), _table_bf16(bc), _table_bf16(bs))


def _s5_disc_kernel(lr_ref, li_ref, ldt_ref, br_ref, bi_ref, lbr_ref, lbi_ref, bbr_ref, bbi_ref):
    lr = jnp.minimum(lr_ref[...], -1e-4)
    li = li_ref[...]
    dt = jnp.exp(ldt_ref[...])
    mag = jnp.exp(lr * dt)
    lbr = mag * jnp.cos(li * dt)
    lbi = mag * jnp.sin(li * dt)
    lbr_ref[...] = lbr
    lbi_ref[...] = lbi
    a, bb = lbr - 1.0, lbi
    den = lr * lr + li * li
    cr = (a * lr + bb * li) / den
    ci = (bb * lr - a * li) / den
    br, bi = br_ref[...], bi_ref[...]
    bbr_ref[...] = cr * br - ci * bi
    bbi_ref[...] = cr * bi + ci * br


def _s5_discretise(lam_re, lam_im, log_dt, b_re, b_im):
    _, g, p, n = b_re.shape
    rows = 2 * g * n
    ex = lambda t: jnp.broadcast_to(t[:, :, None, :], (2, g, n, p)).reshape(rows, p)
    ldt = jnp.broadcast_to(log_dt[:, :, None, None], (2, g, n, p)).reshape(rows, p)
    tb = lambda t: t.transpose(0, 1, 3, 2).reshape(rows, p)
    spec = pl.BlockSpec((rows, p), lambda: (0, 0))
    sds = jax.ShapeDtypeStruct((rows, p), _F32)
    lbr, lbi, bbr, bbi = pl.pallas_call(
        _s5_disc_kernel,
        in_specs=[spec] * 5, out_specs=[spec] * 4, out_shape=[sds] * 4,
        name="s5_discretise",
    )(ex(lam_re), ex(lam_im), ldt, tb(b_re), tb(b_im))
    shp = (2, g, n, p)
    return lbr.reshape(shp)[:, :, 0, :], lbi.reshape(shp)[:, :, 0, :], bbr.reshape(shp), bbi.reshape(shp)


def _s5_scan_kernel(ucf_ref, uxf_ref, ucr_ref, uxr_ref, bf_ref, br_ref, lfr_re_ref, lfr_im_ref, lrf_re_ref,
                    lrf_im_ref, cf_re_ref, cf_im_ref, cr_re_ref, cr_im_ref,
                    yfc_ref, yfx_ref, yrc_ref, yrx_ref,
                    buf_re, buf_im, bur_re, bur_im, hf_re, hf_im, hr_re, hr_im, st_re, st_im, *, ts, nc, nst):
    j = pl.program_id(0)

    @pl.when(j == 0)
    def _():
        st_re[...] = jnp.zeros_like(st_re)
        st_im[...] = jnp.zeros_like(st_im)

    def drive(uf_ref, ur_ref):
        uf, ur = uf_ref[...], ur_ref[...]
        buf_re[...] = _dot(uf, bf_ref[:, 0:nst])
        buf_im[...] = _dot(uf, bf_ref[:, nst:2 * nst])
        bur_re[...] = _dot(ur, br_ref[:, 0:nst])
        bur_im[...] = _dot(ur, br_ref[:, nst:2 * nst])

    pl.when(j < nc)(lambda: drive(ucf_ref, ucr_ref))
    pl.when(j >= nc)(lambda: drive(uxf_ref, uxr_ref))

    lower = lax.broadcasted_iota(jnp.int32, (8, nst), 0) < 4
    ntile = ts // 2

    def cmul_add(l_re, l_im, h_re, h_im, s_re, s_im):
        return l_re * h_re - l_im * h_im + s_re, l_re * h_im + l_im * h_re + s_im

    def pair(m, carry):
        h_re, h_im = carry
        of = pl.multiple_of(m * 8, 8)
        orv = pl.multiple_of((ntile - 1 - m) * 8, 8)
        a_re, a_im = buf_re[pl.ds(of, 8), :], buf_im[pl.ds(of, 8), :]
        b_re, b_im = bur_re[pl.ds(orv, 8), :], bur_im[pl.ds(orv, 8), :]
        e_re, e_im = cmul_add(lfr_re_ref[...], lfr_im_ref[...], h_re, h_im,
                              jnp.where(lower, a_re, b_re), jnp.where(lower, a_im, b_im))
        o_re, o_im = cmul_add(lrf_re_ref[...], lrf_im_ref[...],
                              pltpu.roll(e_re, 4, 0), pltpu.roll(e_im, 4, 0),
                              jnp.where(lower, b_re, a_re), jnp.where(lower, b_im, a_im))
        hf_re[pl.ds(of, 8), :] = jnp.where(lower, e_re, o_re)
        hf_im[pl.ds(of, 8), :] = jnp.where(lower, e_im, o_im)
        hr_re[pl.ds(orv, 8), :] = jnp.where(lower, o_re, e_re)
        hr_im[pl.ds(orv, 8), :] = jnp.where(lower, o_im, e_im)
        return pltpu.roll(o_re, 4, 0), pltpu.roll(o_im, 4, 0)

    h_re, h_im = lax.fori_loop(0, ntile, pair, (st_re[...], st_im[...]))
    st_re[...] = h_re
    st_im[...] = h_im

    yf = _dot(hf_re[...].astype(_BF16), cf_re_ref[...]) - _dot(hf_im[...].astype(_BF16), cf_im_ref[...])
    yr = _dot(hr_re[...].astype(_BF16), cr_re_ref[...]) - _dot(hr_im[...].astype(_BF16), cr_im_ref[...])

    @pl.when(j < nc)
    def _():
        yfc_ref[...] = yf
        yrc_ref[...] = yr

    @pl.when(j >= nc)
    def _():
        yfx_ref[...] = yf
        yrx_ref[...] = yr


def _s5_scan(u_c, u_x, bf, br, lfr_re, lfr_im, lrf_re, lrf_im, cf_re, cf_im, cr_re, cr_im, nb, ts):
    sw = u_x.shape[1]
    nst = lfr_re.shape[1]
    blk = ts * nb
    nc, nx = u_c.shape[0] // blk, u_x.shape[0] // blk
    nt = nc + nx
    full = lambda a: pl.BlockSpec(a.shape, lambda j: (0, 0))
    cf = pl.BlockSpec((blk, sw), lambda j: (jnp.minimum(j, nc - 1), 0))
    xf = pl.BlockSpec((blk, sw), lambda j: (jnp.maximum(j - nc, 0), 0))
    cr = pl.BlockSpec((blk, sw), lambda j: (jnp.maximum(nc - 1 - j, 0), 0))
    xr = pl.BlockSpec((blk, sw), lambda j: (jnp.minimum(nt - 1 - j, nx - 1), 0))
    sds = lambda a: jax.ShapeDtypeStruct(a.shape, _F32)
    return pl.pallas_call(
        functools.partial(_s5_scan_kernel, ts=ts, nc=nc, nst=nst),
        grid=(nt,),
        in_specs=[cf, xf, cr, xr, full(bf), full(br), full(lfr_re), full(lfr_im), full(lrf_re), full(lrf_im),
                  full(cf_re), full(cf_im), full(cr_re), full(cr_im)],
        out_specs=[cf, xf, cr, xr],
        out_shape=[sds(u_c), sds(u_x), sds(u_c), sds(u_x)],
        scratch_shapes=[pltpu.VMEM((blk, nst), _F32)] * 8 + [pltpu.VMEM((8, nst), _F32)] * 2,
        compiler_params=_cparams("arbitrary"),
        name="s5_scan",
    )(u_c, u_x, u_c, u_x, bf, br, lfr_re, lfr_im, lrf_re, lrf_im, cf_re, cf_im, cr_re, cr_im)


def _block_diag(t):
    g, r, c = t.shape
    eye = jnp.eye(g, dtype=t.dtype)
    return (t[:, :, None, :] * eye[:, None, :, None]).reshape(g * r, g * c)


def _s5_mix(zs_x, zs_c, nb, lam_re, lam_im, log_dt, b_re, b_im, c_re, c_im):
    l = zs_x.shape[0]
    lc = zs_c.shape[0]
    sw = zs_x.shape[1] // nb
    assert 2 * nb == 8, "the scan packs batch x direction on the 8 sublanes"
    lbr, lbi, bbr, bbi = _s5_discretise(lam_re, lam_im, log_dt, b_re, b_im)
    nst = lbr.shape[1] * lbr.shape[2]
    drive = lambda d: jnp.concatenate([_block_diag(bbr[d]), _block_diag(bbi[d])], axis=1).astype(_BF16)
    rd = lambda t: _block_diag(t.transpose(0, 2, 1)).astype(_BF16)
    tile = lambda t, first: jnp.concatenate([jnp.broadcast_to(t[first].reshape(1, nst), (nb, nst)),
                                             jnp.broadcast_to(t[1 - first].reshape(1, nst), (nb, nst))], axis=0)
    yf_c, yf_x, yr_c, yr_x = _s5_scan(
        zs_c.reshape(lc * nb, sw), zs_x.reshape(l * nb, sw), drive(0), drive(1),
        tile(lbr, 0), tile(lbi, 0), tile(lbr, 1), tile(lbi, 1),
        rd(c_re[0]), rd(c_im[0]), rd(c_re[1]), rd(c_im[1]), nb, S5_TS)
    tm_major = lambda y, n: y.reshape(n, nb * sw)
    return (tm_major(yf_x, l), tm_major(yr_x, l)), (tm_major(yf_c, lc), tm_major(yr_c, lc))


def _na_bias_tables(rpb, rows):
    w = np.arange(GRID_W)
    kc = np.arange(GRID_W)
    cs = np.clip(w - NA_KW // 2, 0, GRID_W - NA_KW)
    valid_c = (kc[None, :] >= cs[:, None]) & (kc[None, :] < cs[:, None] + NA_KW)
    idx_c = np.clip(kc[None, :] - w[:, None] + NA_KW - 1, 0, 2 * NA_KW - 2)
    r = np.concatenate([np.arange(NA_QROWS), [NA_QROWS], rows - NA_QROWS + np.arange(NA_QROWS)])
    rs = np.clip(r - NA_KH // 2, 0, rows - NA_KH)
    idx_r = rs[:, None] + np.arange(NA_KH)[None, :] - r[:, None] + NA_KH - 1
    t = rpb[:, idx_r][..., idx_c]
    t = jnp.where(valid_c[None, None, None], t, NEG_BIG)
    h, e = rpb.shape[0], r.shape[0]
    return t.transpose(0, 1, 3, 2, 4).reshape(h, e, GRID_W, NA_KH * GRID_W)


def _na_kernel(q_ref, k_ref, v_ref, kc_ref, vc_ref, t_ref, o_ref, *, rows, nb):
    blk = pl.program_id(2)
    dh = NA_HEAD_DIM
    win = NA_KH * GRID_W
    q = q_ref[...]
    head0 = lax.broadcasted_iota(jnp.int32, q.shape, 1) < dh
    starts, entries = [], []
    for dr in range(NA_QROWS):
        r = blk * NA_QROWS + dr
        rs = jnp.clip(r - NA_KH // 2, 0, rows - NA_KH)
        starts.append(pl.multiple_of(rs * GRID_W, GRID_W))
        entries.append(jnp.where(blk == 0, dr, jnp.where(blk == nb - 1, NA_QROWS + 1 + dr, NA_QROWS)))
    outs = []
    for h in range(2):
        qh = jnp.where(head0 if h == 0 else jnp.logical_not(head0), q, jnp.zeros_like(q))
        s_ctx = _dot_nt(qh, kc_ref[...])
        s_loc = jnp.concatenate(
            [_dot_nt(qh[dr * GRID_W:(dr + 1) * GRID_W], k_ref[pl.ds(starts[dr], win), :]) + t_ref[h, entries[dr]]
             for dr in range(NA_QROWS)], axis=0)
        m = jnp.maximum(jnp.max(s_loc, axis=-1, keepdims=True), jnp.max(s_ctx, axis=-1, keepdims=True))
        p_loc = jnp.exp(s_loc - m)
        p_ctx = jnp.exp(s_ctx - m)
        den = jnp.sum(p_loc, axis=-1, keepdims=True) + jnp.sum(p_ctx, axis=-1, keepdims=True)
        p_loc = p_loc.astype(_BF16)
        o = jnp.concatenate(
            [_dot(p_loc[dr * GRID_W:(dr + 1) * GRID_W], v_ref[pl.ds(starts[dr], win), :])
             for dr in range(NA_QROWS)], axis=0)
        o = o + _dot(p_ctx.astype(_BF16), vc_ref[...])
        outs.append(o / den)
    o_ref[...] = jnp.where(head0, outs[0], outs[1]).astype(_BF16)


def _na_latent(q, k, v, kc, vc, rpb):
    b, l, aw = q.shape
    lc = kc.shape[1]
    rows = l // GRID_W
    assert rows % NA_QROWS == 0 and rows >= 2 * NA_QROWS
    nb = rows // NA_QROWS
    hp = aw // (2 * NA_HEAD_DIM)
    tq = NA_QROWS * GRID_W
    tables = _na_bias_tables(rpb.astype(_F32), rows)
    tables = tables.reshape((hp, 2) + tables.shape[1:])
    return pl.pallas_call(
        functools.partial(_na_kernel, rows=rows, nb=nb),
        grid=(hp, b, nb),
        in_specs=[pl.BlockSpec((None, tq, 2 * NA_HEAD_DIM), lambda p, bi, j: (bi, j, p)),
                  pl.BlockSpec((None, l, 2 * NA_HEAD_DIM), lambda p, bi, j: (bi, 0, p)),
                  pl.BlockSpec((None, l, 2 * NA_HEAD_DIM), lambda p, bi, j: (bi, 0, p)),
                  pl.BlockSpec((None, lc, 2 * NA_HEAD_DIM), lambda p, bi, j: (bi, 0, p)),
                  pl.BlockSpec((None, lc, 2 * NA_HEAD_DIM), lambda p, bi, j: (bi, 0, p)),
                  pl.BlockSpec((None,) + tables.shape[1:], lambda p, bi, j: (p, 0, 0, 0, 0))],
        out_specs=pl.BlockSpec((None, tq, 2 * NA_HEAD_DIM), lambda p, bi, j: (bi, j, p)),
        out_shape=jax.ShapeDtypeStruct((b, l, aw), _BF16),
        compiler_params=_cparams("parallel", "parallel", "arbitrary"),
        name="na_latent",
    )(q, k, v, kc, vc, tables)


def _ctx_attn_kernel(q_ref, k_ref, v_ref, o_ref):
    dh = NA_HEAD_DIM
    outs = []
    for h in range(2):
        ls = slice(h * dh, (h + 1) * dh)
        s = _dot_nt(q_ref[:, ls], k_ref[:, ls])
        m = jnp.max(s, axis=-1, keepdims=True)
        p = jnp.exp(s - m)
        den = jnp.sum(p, axis=-1, keepdims=True)
        outs.append(_dot(p.astype(_BF16), v_ref[:, ls]) / den)
    o_ref[...] = jnp.concatenate(outs, axis=-1).astype(_BF16)


def _ctx_attention(q, k, v):
    b, lc, aw = q.shape
    hp = aw // (2 * NA_HEAD_DIM)
    spec = pl.BlockSpec((None, lc, 2 * NA_HEAD_DIM), lambda bi, p: (bi, 0, p))
    return pl.pallas_call(
        _ctx_attn_kernel,
        grid=(b, hp),
        in_specs=[spec] * 3, out_specs=spec,
        out_shape=jax.ShapeDtypeStruct((b, lc, aw), _BF16),
        compiler_params=_cparams("parallel", "parallel"),
        name="ctx_attention",
    )(q, k, v)


def _gelu_tanh(x):
    return 0.5 * x * (1.0 + jnp.tanh(math.sqrt(2.0 / math.pi) * (x + 0.044715 * (x * x * x))))


def _merge_kernel(x_ref, mod_ref, g_ref, fa_ref, u_ref, yf_ref, yr_ref, nc_ref, d_ref, wglu_ref,
                  wg_ref, wa_ref, wb_ref, wc_ref, wo_ref, o_ref):
    x = x_ref[...]
    d = x.shape[1]
    h = _norm_mod(x, g_ref[...], mod_ref[0:1, :], mod_ref[1:2, :]).astype(_BF16)
    y = u_ref[...].astype(_F32) * d_ref[...] + yf_ref[...] + yr_ref[...]
    gl = _gelu_tanh(y)
    sb = gl * jax.nn.sigmoid(_dot(gl.astype(_BF16), wglu_ref[...]))
    m = jax.nn.sigmoid(_dot(h, wg_ref[:, 0:d])) * _dot(fa_ref[...], wa_ref[...])
    m += jax.nn.sigmoid(_dot(h, wg_ref[:, d:2 * d])) * _dot(sb.astype(_BF16), wb_ref[...])
    m += jax.nn.sigmoid(_dot(h, wg_ref[:, 2 * d:3 * d])) * _dot(nc_ref[...], wc_ref[...])
    o_ref[...] = x + mod_ref[2:3, :] * _dot(m.astype(_BF16), wo_ref[...])


def _merge(x, mod, g, fa, u, yf, yr, nc, d_skip, w_glu, w_gate, w_a, w_b, w_c, w_out, tm):
    b, l, d = x.shape
    tok = lambda a: pl.BlockSpec((None, tm, a.shape[2]), lambda bi, i: (bi, i, 0))
    tmaj = lambda a: pl.BlockSpec((tm, a.shape[1] // b), lambda bi, i: (i, bi))
    full = lambda a: pl.BlockSpec(a.shape, lambda bi, i: (0, 0))
    return pl.pallas_call(
        _merge_kernel,
        grid=(b, l // tm),
        in_specs=[tok(x), pl.BlockSpec((None, 6, d), lambda bi, i: (bi, 0, 0)), full(g),
                  tok(fa), tmaj(u), tmaj(yf), tmaj(yr), tok(nc), full(d_skip), full(w_glu),
                  full(w_gate), full(w_a), full(w_b), full(w_c), full(w_out)],
        out_specs=tok(x),
        out_shape=jax.ShapeDtypeStruct((b, l, d), _F32),
        compiler_params=_cparams("parallel", "parallel"),
        name="merge",
    )(x, mod, g, fa, u, yf, yr, nc, d_skip, w_glu, w_gate, w_a, w_b, w_c, w_out)


def _ffn_kernel(x_ref, mod_ref, g_ref, w1_ref, w2_ref, gf_ref, o_ref, *, final_norm, fchunk):
    x = x_ref[...]
    h = _norm_mod(x, g_ref[...], mod_ref[3:4, :], mod_ref[4:5, :]).astype(_BF16)
    dff = w1_ref.shape[1]
    acc = jnp.zeros_like(x)
    for c in range(dff // fchunk):
        cs = slice(c * fchunk, (c + 1) * fchunk)
        t = jnp.maximum(_dot(h, w1_ref[:, cs]), 0.0)
        acc += _dot((t * t).astype(_BF16), w2_ref[cs, :])
    x2 = x + mod_ref[5:6, :] * acc
    if final_norm:
        ms = jnp.mean(x2 * x2, axis=-1, keepdims=True)
        x2 = x2 * lax.rsqrt(ms + RMS_EPS) * gf_ref[...]
    o_ref[...] = x2


def _ffn(x, mod, g, w1, w2, g_final, final_norm, tm):
    b, l, d = x.shape
    tok = pl.BlockSpec((None, tm, d), lambda bi, i: (bi, i, 0))
    full = lambda a: pl.BlockSpec(a.shape, lambda bi, i: (0, 0))
    return pl.pallas_call(
        functools.partial(_ffn_kernel, final_norm=final_norm, fchunk=1024),
        grid=(b, l // tm),
        in_specs=[tok, pl.BlockSpec((None, 6, d), lambda bi, i: (bi, 0, 0)), full(g),
                  full(w1), full(w2), full(g_final)],
        out_specs=tok,
        out_shape=jax.ShapeDtypeStruct((b, l, d), _F32),
        compiler_params=_cparams("parallel", "parallel"),
        name="ffn",
    )(x, mod, g, w1, w2, g_final)


def kernel(x, c, ctx, c_ctx, w_mod, b_mod, g_norm1, g_norm2, w_in, w_br_a, w_br_b, w_br_c, w_out, s5_lam_re, s5_lam_im, s5_log_dt, s5_b_re, s5_b_im, s5_c_re, s5_c_im, s5_d, s5_w_glu, na_rpb, w_ff1, w_ff2, g_final):
    b, l, d = x.shape
    lc = ctx.shape[1]
    depth = w_mod.shape[0]
    fw = w_br_a.shape[1]
    sw = w_br_b.shape[1]
    aw = w_br_c.shape[1]
    off_g = fw + sw + 3 * aw
    tm = 512
    tmc = lc

    c8 = jnp.concatenate([c, c_ctx[None, :], jnp.zeros((8 - b - 1, d), _F32)], axis=0)
    mod = _modulation(c8, w_mod, b_mod).reshape(depth, 8, 6, d)
    gf = g_final.reshape(1, d)

    for i in range(depth):
        need_ctx_out = i < depth - 1
        mod_x = mod[i, :b]
        mod_c = jnp.broadcast_to(mod[i, b][None], (b, 6, d))
        g1 = g_norm1[i].reshape(1, d)
        g2 = g_norm2[i].reshape(1, d)
        w_proj = w_in[i, :, :off_g].astype(_BF16)
        w_gate = w_in[i, :, off_g:].astype(_BF16)
        widths = (fw, sw, aw)

        zf_x, zs_x, q_x, k_x, v_x = _inproj(x, mod_x, g1, w_proj, widths, tm)
        zf_c, zs_c, q_c, k_c, v_c = _inproj(ctx, mod_c, g1, w_proj, widths, tmc)

        fa_x = _fnet_latent(zf_x)
        (yf_x, yr_x), (yf_c, yr_c) = _s5_mix(zs_x, zs_c, b, s5_lam_re[i], s5_lam_im[i], s5_log_dt[i],
                                              s5_b_re[i], s5_b_im[i], s5_c_re[i], s5_c_im[i])
        nc_x = _na_latent(q_x, k_x, v_x, k_c, v_c, na_rpb[i])

        branch_w = (s5_d[i].reshape(1, sw), s5_w_glu[i].astype(_BF16), w_gate, w_br_a[i].astype(_BF16),
                    w_br_b[i].astype(_BF16), w_br_c[i].astype(_BF16), w_out[i].astype(_BF16))
        w1 = w_ff1[i].astype(_BF16)
        w2 = w_ff2[i].astype(_BF16)

        x = _merge(x, mod_x, g1, fa_x, zs_x, yf_x, yr_x, nc_x, *branch_w, tm)
        x = _ffn(x, mod_x, g2, w1, w2, gf, not need_ctx_out, tm)

        if need_ctx_out:
            fa_c = _fnet_ctx(zf_c)
            nc_c = _ctx_attention(q_c, k_c, v_c)
            ctx = _merge(ctx, mod_c, g1, fa_c, zs_c, yf_c, yr_c, nc_c, *branch_w, tmc)
            ctx = _ffn(ctx, mod_c, g2, w1, w2, gf, False, tmc)
    return x
```

```python
import functools
import math

import numpy as np
import jax
import jax.numpy as jnp
from jax import lax
from jax.experimental import pallas as pl
from jax.experimental.pallas import tpu as pltpu

_F32 = jnp.float32
_BF16 = jnp.bfloat16

RMS_EPS = 1e-6
GRID_W = 64
F_GROUP_DIM = 64
S5_GROUP_DIM = 16
NA_HEAD_DIM = 64
NA_KH = 8
NA_KW = 16
NA_QROWS = 8
NEG_BIG = -1e30
FFT_L2 = 128
S5_TS = 64

VMEM_LIMIT_BYTES = 56 * 1024 * 1024


def _cparams(*sem):
    return pltpu.CompilerParams(dimension_semantics=sem, vmem_limit_bytes=VMEM_LIMIT_BYTES)


def _dot(a, b):
    return jnp.dot(a, b, preferred_element_type=_F32)


def _table_bf16(t):
    return jnp.asarray(t, _F32).astype(_BF16)


def _dot_nt(a, b):
    return lax.dot_general(a, b, (((1,), (1,)), ((), ())), preferred_element_type=_F32)


def _norm_mod(x, g, shift, scale):
    ms = jnp.mean(x * x, axis=-1, keepdims=True)
    y = x * lax.rsqrt(ms + RMS_EPS) * g
    return y * (1.0 + scale) + shift


def _mod_kernel(c_ref, w_ref, b_ref, o_ref):
    c = c_ref[...]
    s = c * jax.nn.sigmoid(c)
    o_ref[...] = jnp.dot(s, w_ref[...], precision=lax.Precision.HIGHEST,
                         preferred_element_type=_F32) + b_ref[...]


def _modulation(c8, w_mod, b_mod):
    depth, d, n = w_mod.shape
    tn = 1024
    return pl.pallas_call(
        _mod_kernel,
        grid=(depth, n // tn),
        in_specs=[pl.BlockSpec((8, d), lambda l, j: (0, 0)),
                  pl.BlockSpec((None, d, tn), lambda l, j: (l, 0, j)),
                  pl.BlockSpec((None, 1, tn), lambda l, j: (l, 0, j))],
        out_specs=pl.BlockSpec((None, 8, tn), lambda l, j: (l, 0, j)),
        out_shape=jax.ShapeDtypeStruct((depth, 8, n), _F32),
        compiler_params=_cparams("parallel", "parallel"),
        name="modulation",
    )(c8, w_mod, b_mod.reshape(depth, 1, n))


def _inproj_kernel(x_ref, mod_ref, g_ref, w_ref, zf_ref, zs_ref, q_ref, k_ref, v_ref, *, widths, q_scale):
    h = _norm_mod(x_ref[...], g_ref[...], mod_ref[0:1, :], mod_ref[1:2, :]).astype(_BF16)
    fw, sw, aw = widths
    o = 0
    zf_ref[...] = _dot(h, w_ref[:, o:o + fw]).astype(_BF16)
    o += fw
    zs_ref[...] = _dot(h, w_ref[:, o:o + sw]).astype(_BF16)
    o += sw
    q_ref[...] = (_dot(h, w_ref[:, o:o + aw]) * q_scale).astype(_BF16)
    o += aw
    k_ref[...] = _dot(h, w_ref[:, o:o + aw]).astype(_BF16)
    o += aw
    v_ref[...] = _dot(h, w_ref[:, o:o + aw]).astype(_BF16)


def _inproj(x, mod, g, w, widths, tm):
    b, l, d = x.shape
    fw, sw, aw = widths
    n = w.shape[1]
    tok = lambda width: pl.BlockSpec((None, tm, width), lambda bi, i: (bi, i, 0))
    return pl.pallas_call(
        functools.partial(_inproj_kernel, widths=widths, q_scale=NA_HEAD_DIM ** -0.5),
        grid=(b, l // tm),
        in_specs=[tok(d),
                  pl.BlockSpec((None, 6, d), lambda bi, i: (bi, 0, 0)),
                  pl.BlockSpec((1, d), lambda bi, i: (0, 0)),
                  pl.BlockSpec((d, n), lambda bi, i: (0, 0))],
        out_specs=[tok(fw), pl.BlockSpec((tm, sw), lambda bi, i: (i, bi)), tok(aw), tok(aw), tok(aw)],
        out_shape=[jax.ShapeDtypeStruct((b, l, fw), _BF16), jax.ShapeDtypeStruct((l, b * sw), _BF16),
                   jax.ShapeDtypeStruct((b, l, aw), _BF16), jax.ShapeDtypeStruct((b, l, aw), _BF16),
                   jax.ShapeDtypeStruct((b, l, aw), _BF16)],
        compiler_params=_cparams("parallel", "parallel"),
        name="inproj",
    )(x, mod, g, w)


def _dft_tables(l):
    l1, l2 = l // FFT_L2, FFT_L2
    a1 = 2 * np.pi * np.outer(np.arange(l1), np.arange(l1)) / l1
    f1 = np.concatenate([np.cos(a1), np.sin(a1)], axis=0)
    at = 2 * np.pi * np.outer(np.arange(l1), np.arange(l2)) / l
    a2 = 2 * np.pi * np.outer(np.arange(l2), np.arange(l2)) / l2
    c2, s2 = np.cos(a2), np.sin(a2)
    f2 = np.block([[c2, -s2], [s2, c2]])
    return f1, np.cos(at), np.sin(at), f2


def _group_dft_tables(width):
    ac = 2 * np.pi * np.outer(np.arange(F_GROUP_DIM), np.arange(F_GROUP_DIM)) / F_GROUP_DIM
    eye = np.eye(width // F_GROUP_DIM)
    return np.kron(eye, np.cos(ac)), np.kron(eye, np.sin(ac))


def _fnet_a_kernel(x_ref, f1_ref, cw_ref, sw_ref, o_ref):
    a = _dot(f1_ref[...], x_ref[...])
    l1 = a.shape[0] // 2
    ac, asn = a[:l1], a[l1:]
    cw, sw = cw_ref[...], sw_ref[...]
    o_ref[0] = (ac * cw - asn * sw).astype(_BF16)
    o_ref[1] = (ac * sw + asn * cw).astype(_BF16)


def _fnet_b_kernel(a_ref, f2_ref, bc_ref, bs_ref, o_ref, *, scale, nk1, width):
    half = f2_ref.shape[0] // 2
    for j in range(nk1):
        rhs = jnp.concatenate([a_ref[0, j], a_ref[1, j]], axis=0)
        r = _dot(f2_ref[...], rhs)
        out = _dot(r[:half].astype(_BF16), bc_ref[...]) - _dot(r[half:].astype(_BF16), bs_ref[...])
        o_ref[:, j * width:(j + 1) * width] = (out * scale).astype(_BF16)


def _fnet_latent(zf):
    b, l, width = zf.shape
    l1, l2 = l // FFT_L2, FFT_L2
    f1, cw, sw, f2 = _dft_tables(l)
    bc, bs = _group_dft_tables(width)
    nc = 16 * width
    cw = jnp.repeat(jnp.asarray(cw, _F32), width, axis=1)
    sw = jnp.repeat(jnp.asarray(sw, _F32), width, axis=1)
    a = pl.pallas_call(
        _fnet_a_kernel,
        grid=(l2 * width // nc, b),
        in_specs=[pl.BlockSpec((None, l1, nc), lambda j, bi: (bi, 0, j)),
                  pl.BlockSpec((2 * l1, l1), lambda j, bi: (0, 0)),
                  pl.BlockSpec((l1, nc), lambda j, bi: (0, j)),
                  pl.BlockSpec((l1, nc), lambda j, bi: (0, j))],
        out_specs=pl.BlockSpec((None, 2, l1, nc), lambda j, bi: (bi, 0, 0, j)),
        out_shape=jax.ShapeDtypeStruct((b, 2, l1, l2 * width), _BF16),
        compiler_params=_cparams("parallel", "parallel"),
        name="fnet_a",
    )(zf.reshape(b, l1, l2 * width), _table_bf16(f1), cw, sw)
    nk1 = 8
    out = pl.pallas_call(
        functools.partial(_fnet_b_kernel, scale=float((l * F_GROUP_DIM) ** -0.5), nk1=nk1, width=width),
        grid=(b, l1 // nk1),
        in_specs=[pl.BlockSpec((None, 2, nk1, l2, width), lambda bi, j: (bi, 0, j, 0, 0)),
                  pl.BlockSpec((2 * l2, 2 * l2), lambda bi, j: (0, 0)),
                  pl.BlockSpec((width, width), lambda bi, j: (0, 0)),
                  pl.BlockSpec((width, width), lambda bi, j: (0, 0))],
        out_specs=pl.BlockSpec((None, l2, nk1 * width), lambda bi, j: (bi, 0, j)),
        out_shape=jax.ShapeDtypeStruct((b, l2, l1 * width), _BF16),
        compiler_params=_cparams("parallel", "parallel"),
        name="fnet_b",
    )(a.reshape(b, 2, l1, l2, width), _table_bf16(f2), _table_bf16(bc), _table_bf16(bs))
    return out.reshape(b, l, width)


def _fnet_ctx_kernel(z_ref, cl_ref, sl_ref, bc_ref, bs_ref, o_ref, *, scale):
    z = z_ref[...]
    zc = _dot(z, bc_ref[...]).astype(_BF16)
    zs = _dot(z, bs_ref[...]).astype(_BF16)
    o_ref[...] = ((_dot(cl_ref[...], zc) - _dot(sl_ref[...], zs)) * scale).astype(_BF16)


def _fnet_ctx(zf):
    b, l, width = zf.shape
    al = 2 * np.pi * np.outer(np.arange(l), np.arange(l)) / l
    bc, bs = _group_dft_tables(width)
    full = lambda r, c: pl.BlockSpec((r, c), lambda bi: (0, 0))
    return pl.pallas_call(
        functools.partial(_fnet_ctx_kernel, scale=float((l * F_GROUP_DIM) ** -0.5)),
        grid=(b,),
        in_specs=[pl.BlockSpec((None, l, width), lambda bi: (bi, 0, 0)),
                  full(l, l), full(l, l), full(width, width), full(width, width)],
        out_specs=pl.BlockSpec((None, l, width), lambda bi: (bi, 0, 0)),
        out_shape=jax.ShapeDtypeStruct((b, l, width), _BF16),
        compiler_params=_cparams("parallel"),
        name="fnet_ctx",
    )(zf, _table_bf16(np.cos(al)), _table_bf16(np.sin(al)), _table_bf16(bc), _table_bf16(bs))


def _s5_disc_kernel(lr_ref, li_ref, ldt_ref, br_ref, bi_ref, lbr_ref, lbi_ref, bbr_ref, bbi_ref):
    lr = jnp.minimum(lr_ref[...], -1e-4)
    li = li_ref[...]
    dt = jnp.exp(ldt_ref[...])
    mag = jnp.exp(lr * dt)
    lbr = mag * jnp.cos(li * dt)
    lbi = mag * jnp.sin(li * dt)
    lbr_ref[...] = lbr
    lbi_ref[...] = lbi
    a, bb = lbr - 1.0, lbi
    den = lr * lr + li * li
    cr = (a * lr + bb * li) / den
    ci = (bb * lr - a * li) / den
    br, bi = br_ref[...], bi_ref[...]
    bbr_ref[...] = cr * br - ci * bi
    bbi_ref[...] = cr * bi + ci * br


def _s5_discretise(lam_re, lam_im, log_dt, b_re, b_im):
    _, g, p, n = b_re.shape
    rows = 2 * g * n
    ex = lambda t: jnp.broadcast_to(t[:, :, None, :], (2, g, n, p)).reshape(rows, p)
    ldt = jnp.broadcast_to(log_dt[:, :, None, None], (2, g, n, p)).reshape(rows, p)
    tb = lambda t: t.transpose(0, 1, 3, 2).reshape(rows, p)
    spec = pl.BlockSpec((rows, p), lambda: (0, 0))
    sds = jax.ShapeDtypeStruct((rows, p), _F32)
    lbr, lbi, bbr, bbi = pl.pallas_call(
        _s5_disc_kernel,
        in_specs=[spec] * 5, out_specs=[spec] * 4, out_shape=[sds] * 4,
        name="s5_discretise",
    )(ex(lam_re), ex(lam_im), ldt, tb(b_re), tb(b_im))
    shp = (2, g, n, p)
    return lbr.reshape(shp)[:, :, 0, :], lbi.reshape(shp)[:, :, 0, :], bbr.reshape(shp), bbi.reshape(shp)


def _s5_scan_kernel(ucf_ref, uxf_ref, ucr_ref, uxr_ref, perm_ref, permt_ref, bf_ref, br_ref, l_re_ref, l_im_ref,
                    cf_re_ref, cf_im_ref, cr_re_ref, cr_im_ref,
                    yfc_ref, yfx_ref, yrc_ref, yrx_ref,
                    d0a, d0b, d0c, d0d, d1a, d1b, d1c, d1d, h0a, h0b, h0c, h0d, h1a, h1b, h1c, h1d,
                    st_re, st_im, *, ts, nb, nc2, nst, sw):
    g = pl.program_id(0)
    d0, d1 = (d0a, d0b, d0c, d0d), (d1a, d1b, d1c, d1d)
    h0, h1 = (h0a, h0b, h0c, h0d), (h1a, h1b, h1c, h1d)

    @pl.when(g == 0)
    def _():
        for r in d0 + d1 + h0 + h1 + (st_re, st_im):
            r[...] = jnp.zeros_like(r)

    ntile = ts // 2
    lower = lax.broadcasted_iota(jnp.int32, (8, nst), 0) < nb
    is_ctx_in = g < nc2
    u_f = jnp.where(is_ctx_in, ucf_ref[...], uxf_ref[...])
    u_r = jnp.where(is_ctx_in, ucr_ref[...], uxr_ref[...])

    def time_batch_rows(u):
        stack = jnp.concatenate([u[:, b * sw:(b + 1) * sw] for b in range(nb)], axis=0)
        return _dot(perm_ref[...], stack).astype(_BF16)

    def drive(uf, ur, d):
        uf, ur = time_batch_rows(uf), time_batch_rows(ur)
        d[0][...] = _dot(uf, bf_ref[:, 0:nst])
        d[1][...] = _dot(uf, bf_ref[:, nst:2 * nst])
        d[2][...] = _dot(ur, br_ref[:, 0:nst])
        d[3][...] = _dot(ur, br_ref[:, nst:2 * nst])

    def scan(d, h):
        l_re, l_im = l_re_ref[...], l_im_ref[...]
        x_re, x_im = st_re[...], st_im[...]

        def step(x_re, x_im, s_re, s_im):
            return l_re * x_re - l_im * x_im + s_re, l_re * x_im + l_im * x_re + s_im

        for k in range(ntile // 2):
            f_tiles, r_tiles = [], []
            for m in (2 * k, 2 * k + 1):
                tf = slice(8 * m, 8 * m + 8)
                tr = slice(8 * (ntile - 1 - m), 8 * (ntile - m))
                a_re, a_im, b_re, b_im = d[0][tf, :], d[1][tf, :], d[2][tr, :], d[3][tr, :]
                e_re, e_im = step(x_re, x_im, jnp.where(lower, a_re, b_re), jnp.where(lower, a_im, b_im))
                x_re, x_im = step(e_re, e_im, pltpu.roll(jnp.where(lower, b_re, a_re), nb, 0),
                                  pltpu.roll(jnp.where(lower, b_im, a_im), nb, 0))
                o_re, o_im = pltpu.roll(x_re, nb, 0), pltpu.roll(x_im, nb, 0)
                f_tiles.append((jnp.where(lower, e_re, o_re), jnp.where(lower, e_im, o_im)))
                r_tiles.append((jnp.where(lower, o_re, e_re), jnp.where(lower, o_im, e_im)))
            pf = slice(16 * k, 16 * k + 16)
            pr = slice(16 * (ntile // 2 - 1 - k), 16 * (ntile // 2 - k))
            pack = lambda lo_t, hi_t: jnp.concatenate([lo_t, hi_t], axis=0).astype(_BF16)
            h[0][pf, :] = pack(f_tiles[0][0], f_tiles[1][0])
            h[1][pf, :] = pack(f_tiles[0][1], f_tiles[1][1])
            h[2][pr, :] = pack(r_tiles[1][0], r_tiles[0][0])
            h[3][pr, :] = pack(r_tiles[1][1], r_tiles[0][1])
        st_re[...] = x_re
        st_im[...] = x_im

    def readout(h):
        yf = (_dot(h[0][...], cf_re_ref[...]) - _dot(h[1][...], cf_im_ref[...])).astype(_BF16)
        yr = (_dot(h[2][...], cr_re_ref[...]) - _dot(h[3][...], cr_im_ref[...])).astype(_BF16)
        return _dot(permt_ref[...], yf).astype(_BF16), _dot(permt_ref[...], yr).astype(_BF16)

    lo, hi = slice(0, ts), slice(ts, 2 * ts)
    scan(d1, h1)
    drive(u_f[lo], u_r[hi], d0)
    yf0, yr0 = readout(h0)
    drive(u_f[hi], u_r[lo], d1)
    scan(d0, h0)
    yf1, yr1 = readout(h1)

    def store(yf_ref, yr_ref):
        for b in range(nb):
            cs = slice(b * sw, (b + 1) * sw)
            rows = slice(b * ts, (b + 1) * ts)
            yf_ref[lo, cs] = yf0[rows]
            yf_ref[hi, cs] = yf1[rows]
            yr_ref[hi, cs] = yr0[rows]
            yr_ref[lo, cs] = yr1[rows]

    pl.when(jnp.logical_and(g >= 1, g <= nc2))(lambda: store(yfc_ref, yrc_ref))
    pl.when(g > nc2)(lambda: store(yfx_ref, yrx_ref))


def _s5_scan(u_c, u_x, bf, br, l_re, l_im, cf_re, cf_im, cr_re, cr_im, nb, ts):
    sw = u_x.shape[1] // nb
    nst = l_re.shape[1]
    blk = 2 * ts
    nc2, nx2 = u_c.shape[0] // blk, u_x.shape[0] // blk
    nt2 = nc2 + nx2
    perm = np.zeros((ts * nb, ts * nb), np.float32)
    for t in range(ts):
        for b in range(nb):
            perm[t * nb + b, b * ts + t] = 1.0
    full = lambda a: pl.BlockSpec(a.shape, lambda g: (0, 0))
    spec = lambda f: pl.BlockSpec((blk, nb * sw), lambda g: (f(g), 0))
    clip = lambda v, n: jnp.clip(v, 0, n - 1)
    ins = [spec(lambda g: clip(g, nc2)), spec(lambda g: clip(g - nc2, nx2)),
           spec(lambda g: clip(nc2 - 1 - g, nc2)), spec(lambda g: clip(nt2 - 1 - g, nx2))]
    outs = [spec(lambda g: clip(g - 1, nc2)), spec(lambda g: clip(g - 1 - nc2, nx2)),
            spec(lambda g: clip(nc2 - g, nc2)), spec(lambda g: clip(nt2 - g, nx2))]
    perm_b, permt_b = _table_bf16(perm), _table_bf16(perm.T)
    sds = lambda a: jax.ShapeDtypeStruct(a.shape, _BF16)
    return pl.pallas_call(
        functools.partial(_s5_scan_kernel, ts=ts, nb=nb, nc2=nc2, nst=nst, sw=sw),
        grid=(nt2 + 1,),
        in_specs=ins + [full(perm_b), full(permt_b), full(bf), full(br), full(l_re), full(l_im),
                        full(cf_re), full(cf_im), full(cr_re), full(cr_im)],
        out_specs=outs,
        out_shape=[sds(u_c), sds(u_x), sds(u_c), sds(u_x)],
        scratch_shapes=([pltpu.VMEM((ts * nb, nst), _F32)] * 8 + [pltpu.VMEM((ts * nb, nst), _BF16)] * 8
                        + [pltpu.VMEM((8, nst), _F32)] * 2),
        compiler_params=_cparams("arbitrary"),
        name="s5_scan",
    )(u_c, u_x, u_c, u_x, perm_b, permt_b, bf, br, l_re, l_im, cf_re, cf_im, cr_re, cr_im)


def _block_diag(t):
    g, r, c = t.shape
    eye = jnp.eye(g, dtype=t.dtype)
    return (t[:, :, None, :] * eye[:, None, :, None]).reshape(g * r, g * c)


def _s5_mix(zs_x, zs_c, nb, lam_re, lam_im, log_dt, b_re, b_im, c_re, c_im):
    assert 2 * nb == 8, "the scan packs batch x direction on the 8 sublanes"
    lbr, lbi, bbr, bbi = _s5_discretise(lam_re, lam_im, log_dt, b_re, b_im)
    nst = lbr.shape[1] * lbr.shape[2]
    drive = lambda d: jnp.concatenate([_block_diag(bbr[d]), _block_diag(bbi[d])], axis=1).astype(_BF16)
    rd = lambda t: _block_diag(t.transpose(0, 2, 1)).astype(_BF16)
    tile = lambda t: jnp.concatenate([jnp.broadcast_to(t[0].reshape(1, nst), (nb, nst)),
                                      jnp.broadcast_to(t[1].reshape(1, nst), (nb, nst))], axis=0)
    yf_c, yf_x, yr_c, yr_x = _s5_scan(zs_c, zs_x, drive(0), drive(1), tile(lbr), tile(lbi),
                                      rd(c_re[0]), rd(c_im[0]), rd(c_re[1]), rd(c_im[1]), nb, S5_TS)
    return (yf_x, yr_x), (yf_c, yr_c)


def _na_row_entries(rows):
    r = np.concatenate([np.arange(NA_QROWS), [NA_QROWS], rows - NA_QROWS + np.arange(NA_QROWS)])
    rs = np.clip(r - NA_KH // 2, 0, rows - NA_KH)
    idx = rs[:, None] + np.arange(NA_KH)[None, :] - r[:, None] + NA_KH - 1
    return tuple(tuple(int(v) for v in row) for row in idx)


def _na_column_tables(rpb):
    w = np.arange(GRID_W)
    kc = np.arange(GRID_W)
    cs = np.clip(w - NA_KW // 2, 0, GRID_W - NA_KW)
    valid_c = (kc[None, :] >= cs[:, None]) & (kc[None, :] < cs[:, None] + NA_KW)
    idx_c = np.clip(kc[None, :] - w[:, None] + NA_KW - 1, 0, 2 * NA_KW - 2)
    return jnp.where(valid_c[None, None], rpb[..., idx_c], NEG_BIG)


def _na_kernel(q_ref, k_ref, v_ref, kc_ref, vc_ref, t_ref, o_ref, tab_ref, *, rows, nb, row_entries):
    blk = pl.program_id(2)
    dh = NA_HEAD_DIM
    win = NA_KH * GRID_W

    @pl.when(jnp.logical_and(pl.program_id(1) == 0, blk == 0))
    def _():
        for h in range(2):
            for e, idx in enumerate(row_entries):
                tab_ref[h, e] = jnp.concatenate([t_ref[h, a] for a in idx], axis=1)

    q = q_ref[...]
    head0 = lax.broadcasted_iota(jnp.int32, q.shape, 1) < dh
    starts, entries = [], []
    for dr in range(NA_QROWS):
        r = blk * NA_QROWS + dr
        rs = jnp.clip(r - NA_KH // 2, 0, rows - NA_KH)
        starts.append(pl.multiple_of(rs * GRID_W, GRID_W))
        entries.append(jnp.where(blk == 0, dr, jnp.where(blk == nb - 1, NA_QROWS + 1 + dr, NA_QROWS)))
    outs = []
    for h in range(2):
        qh = jnp.where(head0 if h == 0 else jnp.logical_not(head0), q, jnp.zeros_like(q))
        s_ctx = _dot_nt(qh, kc_ref[...])
        s_loc = jnp.concatenate(
            [_dot_nt(qh[dr * GRID_W:(dr + 1) * GRID_W], k_ref[pl.ds(starts[dr], win), :]) + tab_ref[h, entries[dr]]
             for dr in range(NA_QROWS)], axis=0)
        m = jnp.maximum(jnp.max(s_loc, axis=-1, keepdims=True), jnp.max(s_ctx, axis=-1, keepdims=True))
        p_loc = jnp.exp(s_loc - m)
        p_ctx = jnp.exp(s_ctx - m)
        den = jnp.sum(p_loc, axis=-1, keepdims=True) + jnp.sum(p_ctx, axis=-1, keepdims=True)
        p_loc = p_loc.astype(_BF16)
        o = jnp.concatenate(
            [_dot(p_loc[dr * GRID_W:(dr + 1) * GRID_W], v_ref[pl.ds(starts[dr], win), :])
             for dr in range(NA_QROWS)], axis=0)
        o = o + _dot(p_ctx.astype(_BF16), vc_ref[...])
        outs.append(o / den)
    o_ref[...] = jnp.where(head0, outs[0], outs[1]).astype(_BF16)


def _na_latent(q, k, v, kc, vc, rpb):
    b, l, aw = q.shape
    lc = kc.shape[1]
    rows = l // GRID_W
    assert rows % NA_QROWS == 0 and rows >= 2 * NA_QROWS
    nb = rows // NA_QROWS
    hp = aw // (2 * NA_HEAD_DIM)
    tq = NA_QROWS * GRID_W
    row_entries = _na_row_entries(rows)
    tables = _na_column_tables(rpb.astype(_F32))
    tables = tables.reshape((hp, 2) + tables.shape[1:])
    return pl.pallas_call(
        functools.partial(_na_kernel, rows=rows, nb=nb, row_entries=row_entries),
        grid=(hp, b, nb),
        in_specs=[pl.BlockSpec((None, tq, 2 * NA_HEAD_DIM), lambda p, bi, j: (bi, j, p)),
                  pl.BlockSpec((None, l, 2 * NA_HEAD_DIM), lambda p, bi, j: (bi, 0, p)),
                  pl.BlockSpec((None, l, 2 * NA_HEAD_DIM), lambda p, bi, j: (bi, 0, p)),
                  pl.BlockSpec((None, lc, 2 * NA_HEAD_DIM), lambda p, bi, j: (bi, 0, p)),
                  pl.BlockSpec((None, lc, 2 * NA_HEAD_DIM), lambda p, bi, j: (bi, 0, p)),
                  pl.BlockSpec((None,) + tables.shape[1:], lambda p, bi, j: (p, 0, 0, 0, 0))],
        out_specs=pl.BlockSpec((None, tq, 2 * NA_HEAD_DIM), lambda p, bi, j: (bi, j, p)),
        out_shape=jax.ShapeDtypeStruct((b, l, aw), _BF16),
        scratch_shapes=[pltpu.VMEM((2, len(row_entries), GRID_W, NA_KH * GRID_W), _F32)],
        compiler_params=_cparams("arbitrary", "arbitrary", "arbitrary"),
        name="na_latent",
    )(q, k, v, kc, vc, tables)


def _ctx_attn_kernel(q_ref, k_ref, v_ref, o_ref):
    dh = NA_HEAD_DIM
    outs = []
    for h in range(2):
        ls = slice(h * dh, (h + 1) * dh)
        s = _dot_nt(q_ref[:, ls], k_ref[:, ls])
        m = jnp.max(s, axis=-1, keepdims=True)
        p = jnp.exp(s - m)
        den = jnp.sum(p, axis=-1, keepdims=True)
        outs.append(_dot(p.astype(_BF16), v_ref[:, ls]) / den)
    o_ref[...] = jnp.concatenate(outs, axis=-1).astype(_BF16)


def _ctx_attention(q, k, v):
    b, lc, aw = q.shape
    hp = aw // (2 * NA_HEAD_DIM)
    spec = pl.BlockSpec((None, lc, 2 * NA_HEAD_DIM), lambda bi, p: (bi, 0, p))
    return pl.pallas_call(
        _ctx_attn_kernel,
        grid=(b, hp),
        in_specs=[spec] * 3, out_specs=spec,
        out_shape=jax.ShapeDtypeStruct((b, lc, aw), _BF16),
        compiler_params=_cparams("parallel", "parallel"),
        name="ctx_attention",
    )(q, k, v)


def _gelu_tanh(x):
    return 0.5 * x * (1.0 + jnp.tanh(math.sqrt(2.0 / math.pi) * (x + 0.044715 * (x * x * x))))


def _merge_kernel(x_ref, mod_ref, g_ref, fa_ref, u_ref, yf_ref, yr_ref, nc_ref, d_ref, wglu_ref,
                  wg_ref, wa_ref, wb_ref, wc_ref, wo_ref, o_ref):
    x = x_ref[...]
    d = x.shape[1]
    h = _norm_mod(x, g_ref[...], mod_ref[0:1, :], mod_ref[1:2, :]).astype(_BF16)
    y = u_ref[...].astype(_F32) * d_ref[...] + yf_ref[...].astype(_F32) + yr_ref[...].astype(_F32)
    gl = _gelu_tanh(y)
    sb = gl * jax.nn.sigmoid(_dot(gl.astype(_BF16), wglu_ref[...]))
    m = jax.nn.sigmoid(_dot(h, wg_ref[:, 0:d])) * _dot(fa_ref[...], wa_ref[...])
    m += jax.nn.sigmoid(_dot(h, wg_ref[:, d:2 * d])) * _dot(sb.astype(_BF16), wb_ref[...])
    m += jax.nn.sigmoid(_dot(h, wg_ref[:, 2 * d:3 * d])) * _dot(nc_ref[...], wc_ref[...])
    o_ref[...] = x + mod_ref[2:3, :] * _dot(m.astype(_BF16), wo_ref[...])


def _merge(x, mod, g, fa, u, yf, yr, nc, d_skip, w_glu, w_gate, w_a, w_b, w_c, w_out, tm):
    b, l, d = x.shape
    tok = lambda a: pl.BlockSpec((None, tm, a.shape[2]), lambda bi, i: (bi, i, 0))
    tmaj = lambda a: pl.BlockSpec((tm, a.shape[1] // b), lambda bi, i: (i, bi))
    full = lambda a: pl.BlockSpec(a.shape, lambda bi, i: (0, 0))
    return pl.pallas_call(
        _merge_kernel,
        grid=(b, l // tm),
        in_specs=[tok(x), pl.BlockSpec((None, 6, d), lambda bi, i: (bi, 0, 0)), full(g),
                  tok(fa), tmaj(u), tmaj(yf), tmaj(yr), tok(nc), full(d_skip), full(w_glu),
                  full(w_gate), full(w_a), full(w_b), full(w_c), full(w_out)],
        out_specs=tok(x),
        out_shape=jax.ShapeDtypeStruct((b, l, d), _F32),
        compiler_params=_cparams("parallel", "parallel"),
        name="merge",
    )(x, mod, g, fa, u, yf, yr, nc, d_skip, w_glu, w_gate, w_a, w_b, w_c, w_out)


def _ffn_kernel(x_ref, mod_ref, g_ref, w1_ref, w2_ref, gf_ref, o_ref, *, final_norm, fchunk):
    x = x_ref[...]
    h = _norm_mod(x, g_ref[...], mod_ref[3:4, :], mod_ref[4:5, :]).astype(_BF16)
    dff = w1_ref.shape[1]
    acc = jnp.zeros_like(x)
    for c in range(dff // fchunk):
        cs = slice(c * fchunk, (c + 1) * fchunk)
        t = jnp.maximum(_dot(h, w1_ref[:, cs]), 0.0)
        acc += _dot((t * t).astype(_BF16), w2_ref[cs, :])
    x2 = x + mod_ref[5:6, :] * acc
    if final_norm:
        ms = jnp.mean(x2 * x2, axis=-1, keepdims=True)
        x2 = x2 * lax.rsqrt(ms + RMS_EPS) * gf_ref[...]
    o_ref[...] = x2


def _ffn(x, mod, g, w1, w2, g_final, final_norm, tm):
    b, l, d = x.shape
    tok = pl.BlockSpec((None, tm, d), lambda bi, i: (bi, i, 0))
    full = lambda a: pl.BlockSpec(a.shape, lambda bi, i: (0, 0))
    return pl.pallas_call(
        functools.partial(_ffn_kernel, final_norm=final_norm, fchunk=1024),
        grid=(b, l // tm),
        in_specs=[tok, pl.BlockSpec((None, 6, d), lambda bi, i: (bi, 0, 0)), full(g),
                  full(w1), full(w2), full(g_final)],
        out_specs=tok,
        out_shape=jax.ShapeDtypeStruct((b, l, d), _F32),
        compiler_params=_cparams("parallel", "parallel"),
        name="ffn",
    )(x, mod, g, w1, w2, g_final)


def kernel(x, c, ctx, c_ctx, w_mod, b_mod, g_norm1, g_norm2, w_in, w_br_a, w_br_b, w_br_c, w_out, s5_lam_re, s5_lam_im, s5_log_dt, s5_b_re, s5_b_im, s5_c_re, s5_c_im, s5_d, s5_w_glu, na_rpb, w_ff1, w_ff2, g_final):
    b, l, d = x.shape
    lc = ctx.shape[1]
    depth = w_mod.shape[0]
    fw = w_br_a.shape[1]
    sw = w_br_b.shape[1]
    aw = w_br_c.shape[1]
    off_g = fw + sw + 3 * aw
    tm = 512
    tmc = lc

    c8 = jnp.concatenate([c, c_ctx[None, :], jnp.zeros((8 - b - 1, d), _F32)], axis=0)
    mod = _modulation(c8, w_mod, b_mod).reshape(depth, 8, 6, d)
    gf = g_final.reshape(1, d)

    for i in range(depth):
        need_ctx_out = i < depth - 1
        mod_x = mod[i, :b]
        mod_c = jnp.broadcast_to(mod[i, b][None], (b, 6, d))
        g1 = g_norm1[i].reshape(1, d)
        g2 = g_norm2[i].reshape(1, d)
        w_proj = w_in[i, :, :off_g].astype(_BF16)
        w_gate = w_in[i, :, off_g:].astype(_BF16)
        widths = (fw, sw, aw)

        zf_x, zs_x, q_x, k_x, v_x = _inproj(x, mod_x, g1, w_proj, widths, tm)
        zf_c, zs_c, q_c, k_c, v_c = _inproj(ctx, mod_c, g1, w_proj, widths, tmc)

        fa_x = _fnet_latent(zf_x)
        (yf_x, yr_x), (yf_c, yr_c) = _s5_mix(zs_x, zs_c, b, s5_lam_re[i], s5_lam_im[i], s5_log_dt[i],
                                              s5_b_re[i], s5_b_im[i], s5_c_re[i], s5_c_im[i])
        nc_x = _na_latent(q_x, k_x, v_x, k_c, v_c, na_rpb[i])

        branch_w = (s5_d[i].reshape(1, sw), s5_w_glu[i].astype(_BF16), w_gate, w_br_a[i].astype(_BF16),
                    w_br_b[i].astype(_BF16), w_br_c[i].astype(_BF16), w_out[i].astype(_BF16))
        w1 = w_ff1[i].astype(_BF16)
        w2 = w_ff2[i].astype(_BF16)

        x = _merge(x, mod_x, g1, fa_x, zs_x, yf_x, yr_x, nc_x, *branch_w, tm)
        x = _ffn(x, mod_x, g2, w1, w2, gf, not need_ctx_out, tm)

        if need_ctx_out:
            fa_c = _fnet_ctx(zf_c)
            nc_c = _ctx_attention(q_c, k_c, v_c)
            ctx = _merge(ctx, mod_c, g1, fa_c, zs_c, yf_c, yr_c, nc_c, *branch_w, tmc)
            ctx = _ffn(ctx, mod_c, g2, w1, w2, gf, False, tmc)
    return x
```

```python
import functools
import math

import numpy as np
import jax
import jax.numpy as jnp
from jax import lax
from jax.experimental import pallas as pl
from jax.experimental.pallas import tpu as pltpu

_F32 = jnp.float32
_BF16 = jnp.bfloat16

RMS_EPS = 1e-6
GRID_W = 64
F_GROUP_DIM = 64
S5_GROUP_DIM = 16
NA_HEAD_DIM = 64
NA_KH = 8
NA_KW = 16
NA_QROWS = 8
NEG_BIG = -1e30
FFT_L2 = 128
S5_TS = 64

VMEM_LIMIT_BYTES = 56 * 1024 * 1024


def _cparams(*sem):
    return pltpu.CompilerParams(dimension_semantics=sem, vmem_limit_bytes=VMEM_LIMIT_BYTES)


def _dot(a, b):
    return jnp.dot(a, b, preferred_element_type=_F32)


def _table_bf16(t):
    return jnp.asarray(t, _F32).astype(_BF16)


def _dot_nt(a, b):
    return lax.dot_general(a, b, (((1,), (1,)), ((), ())), preferred_element_type=_F32)


def _norm_mod(x, g, shift, scale):
    ms = jnp.mean(x * x, axis=-1, keepdims=True)
    y = x * lax.rsqrt(ms + RMS_EPS) * g
    return y * (1.0 + scale) + shift


def _mod_kernel(c_ref, w_ref, b_ref, o_ref):
    c = c_ref[...]
    s = c * jax.nn.sigmoid(c)
    o_ref[...] = jnp.dot(s, w_ref[...], precision=lax.Precision.HIGHEST,
                         preferred_element_type=_F32) + b_ref[...]


def _modulation(c8, w_mod, b_mod):
    depth, d, n = w_mod.shape
    tn = 1024
    return pl.pallas_call(
        _mod_kernel,
        grid=(depth, n // tn),
        in_specs=[pl.BlockSpec((8, d), lambda l, j: (0, 0)),
                  pl.BlockSpec((None, d, tn), lambda l, j: (l, 0, j)),
                  pl.BlockSpec((None, 1, tn), lambda l, j: (l, 0, j))],
        out_specs=pl.BlockSpec((None, 8, tn), lambda l, j: (l, 0, j)),
        out_shape=jax.ShapeDtypeStruct((depth, 8, n), _F32),
        compiler_params=_cparams("parallel", "parallel"),
        name="modulation",
    )(c8, w_mod, b_mod.reshape(depth, 1, n))


def _inproj_kernel(x_ref, mod_ref, g_ref, w_ref, zf_ref, zs_ref, q_ref, k_ref, v_ref, *, widths, q_scale):
    h = _norm_mod(x_ref[...], g_ref[...], mod_ref[0:1, :], mod_ref[1:2, :]).astype(_BF16)
    fw, sw, aw = widths
    o = 0
    zf_ref[...] = _dot(h, w_ref[:, o:o + fw]).astype(_BF16)
    o += fw
    zs_ref[...] = _dot(h, w_ref[:, o:o + sw]).astype(_BF16)
    o += sw
    q_ref[...] = (_dot(h, w_ref[:, o:o + aw]) * q_scale).astype(_BF16)
    o += aw
    k_ref[...] = _dot(h, w_ref[:, o:o + aw]).astype(_BF16)
    o += aw
    v_ref[...] = _dot(h, w_ref[:, o:o + aw]).astype(_BF16)


def _inproj(x, mod, g, w, widths, tm):
    b, l, d = x.shape
    fw, sw, aw = widths
    n = w.shape[1]
    tok = lambda width: pl.BlockSpec((None, tm, width), lambda bi, i: (bi, i, 0))
    return pl.pallas_call(
        functools.partial(_inproj_kernel, widths=widths, q_scale=NA_HEAD_DIM ** -0.5),
        grid=(b, l // tm),
        in_specs=[tok(d),
                  pl.BlockSpec((None, 6, d), lambda bi, i: (bi, 0, 0)),
                  pl.BlockSpec((1, d), lambda bi, i: (0, 0)),
                  pl.BlockSpec((d, n), lambda bi, i: (0, 0))],
        out_specs=[tok(fw), pl.BlockSpec((tm, sw), lambda bi, i: (i, bi)), tok(aw), tok(aw), tok(aw)],
        out_shape=[jax.ShapeDtypeStruct((b, l, fw), _BF16), jax.ShapeDtypeStruct((l, b * sw), _BF16),
                   jax.ShapeDtypeStruct((b, l, aw), _BF16), jax.ShapeDtypeStruct((b, l, aw), _BF16),
                   jax.ShapeDtypeStruct((b, l, aw), _BF16)],
        compiler_params=_cparams("parallel", "parallel"),
        name="inproj",
    )(x, mod, g, w)


def _dft_tables(l):
    l1, l2 = l // FFT_L2, FFT_L2
    a1 = 2 * np.pi * np.outer(np.arange(l1), np.arange(l1)) / l1
    f1 = np.concatenate([np.cos(a1), np.sin(a1)], axis=0)
    at = 2 * np.pi * np.outer(np.arange(l1), np.arange(l2)) / l
    a2 = 2 * np.pi * np.outer(np.arange(l2), np.arange(l2)) / l2
    c2, s2 = np.cos(a2), np.sin(a2)
    f2 = np.block([[c2, -s2], [s2, c2]])
    return f1, np.cos(at), np.sin(at), f2


def _group_dft_tables(width):
    ac = 2 * np.pi * np.outer(np.arange(F_GROUP_DIM), np.arange(F_GROUP_DIM)) / F_GROUP_DIM
    eye = np.eye(width // F_GROUP_DIM)
    return np.kron(eye, np.cos(ac)), np.kron(eye, np.sin(ac))


def _fnet_a_kernel(x_ref, f1_ref, cw_ref, sw_ref, o_ref):
    a = _dot(f1_ref[...], x_ref[...])
    l1 = a.shape[0] // 2
    ac, asn = a[:l1], a[l1:]
    cw, sw = cw_ref[...], sw_ref[...]
    o_ref[0] = (ac * cw - asn * sw).astype(_BF16)
    o_ref[1] = (ac * sw + asn * cw).astype(_BF16)


def _fnet_b_kernel(a_ref, f2_ref, bc_ref, bs_ref, o_ref, *, scale, nk1, width):
    half = f2_ref.shape[0] // 2
    for j in range(nk1):
        rhs = jnp.concatenate([a_ref[0, j], a_ref[1, j]], axis=0)
        r = _dot(f2_ref[...], rhs)
        out = _dot(r[:half].astype(_BF16), bc_ref[...]) - _dot(r[half:].astype(_BF16), bs_ref[...])
        o_ref[:, j * width:(j + 1) * width] = (out * scale).astype(_BF16)


def _fnet_latent(zf):
    b, l, width = zf.shape
    l1, l2 = l // FFT_L2, FFT_L2
    f1, cw, sw, f2 = _dft_tables(l)
    bc, bs = _group_dft_tables(width)
    nc = 16 * width
    cw = jnp.repeat(jnp.asarray(cw, _F32), width, axis=1)
    sw = jnp.repeat(jnp.asarray(sw, _F32), width, axis=1)
    a = pl.pallas_call(
        _fnet_a_kernel,
        grid=(l2 * width // nc, b),
        in_specs=[pl.BlockSpec((None, l1, nc), lambda j, bi: (bi, 0, j)),
                  pl.BlockSpec((2 * l1, l1), lambda j, bi: (0, 0)),
                  pl.BlockSpec((l1, nc), lambda j, bi: (0, j)),
                  pl.BlockSpec((l1, nc), lambda j, bi: (0, j))],
        out_specs=pl.BlockSpec((None, 2, l1, nc), lambda j, bi: (bi, 0, 0, j)),
        out_shape=jax.ShapeDtypeStruct((b, 2, l1, l2 * width), _BF16),
        compiler_params=_cparams("parallel", "parallel"),
        name="fnet_a",
    )(zf.reshape(b, l1, l2 * width), _table_bf16(f1), cw, sw)
    nk1 = 8
    out = pl.pallas_call(
        functools.partial(_fnet_b_kernel, scale=float((l * F_GROUP_DIM) ** -0.5), nk1=nk1, width=width),
        grid=(b, l1 // nk1),
        in_specs=[pl.BlockSpec((None, 2, nk1, l2, width), lambda bi, j: (bi, 0, j, 0, 0)),
                  pl.BlockSpec((2 * l2, 2 * l2), lambda bi, j: (0, 0)),
                  pl.BlockSpec((width, width), lambda bi, j: (0, 0)),
                  pl.BlockSpec((width, width), lambda bi, j: (0, 0))],
        out_specs=pl.BlockSpec((None, l2, nk1 * width), lambda bi, j: (bi, 0, j)),
        out_shape=jax.ShapeDtypeStruct((b, l2, l1 * width), _BF16),
        compiler_params=_cparams("parallel", "parallel"),
        name="fnet_b",
    )(a.reshape(b, 2, l1, l2, width), _table_bf16(f2), _table_bf16(bc), _table_bf16(bs))
    return out.reshape(b, l, width)


def _fnet_ctx_kernel(z_ref, cl_ref, sl_ref, bc_ref, bs_ref, o_ref, *, scale):
    z = z_ref[...]
    zc = _dot(z, bc_ref[...]).astype(_BF16)
    zs = _dot(z, bs_ref[...]).astype(_BF16)
    o_ref[...] = ((_dot(cl_ref[...], zc) - _dot(sl_ref[...], zs)) * scale).astype(_BF16)


def _fnet_ctx(zf):
    b, l, width = zf.shape
    al = 2 * np.pi * np.outer(np.arange(l), np.arange(l)) / l
    bc, bs = _group_dft_tables(width)
    full = lambda r, c: pl.BlockSpec((r, c), lambda bi: (0, 0))
    return pl.pallas_call(
        functools.partial(_fnet_ctx_kernel, scale=float((l * F_GROUP_DIM) ** -0.5)),
        grid=(b,),
        in_specs=[pl.BlockSpec((None, l, width), lambda bi: (bi, 0, 0)),
                  full(l, l), full(l, l), full(width, width), full(width, width)],
        out_specs=pl.BlockSpec((None, l, width), lambda bi: (bi, 0, 0)),
        out_shape=jax.ShapeDtypeStruct((b, l, width), _BF16),
        compiler_params=_cparams("parallel"),
        name="fnet_ctx",
    )(zf, _table_bf16(np.cos(al)), _table_bf16(np.sin(al)), _table_bf16(bc), _table_bf16(bs))


def _s5_disc_kernel(lr_ref, li_ref, ldt_ref, br_ref, bi_ref, lbr_ref, lbi_ref, bbr_ref, bbi_ref):
    lr = jnp.minimum(lr_ref[...], -1e-4)
    li = li_ref[...]
    dt = jnp.exp(ldt_ref[...])
    mag = jnp.exp(lr * dt)
    lbr = mag * jnp.cos(li * dt)
    lbi = mag * jnp.sin(li * dt)
    lbr_ref[...] = lbr
    lbi_ref[...] = lbi
    a, bb = lbr - 1.0, lbi
    den = lr * lr + li * li
    cr = (a * lr + bb * li) / den
    ci = (bb * lr - a * li) / den
    br, bi = br_ref[...], bi_ref[...]
    bbr_ref[...] = cr * br - ci * bi
    bbi_ref[...] = cr * bi + ci * br


def _s5_discretise(lam_re, lam_im, log_dt, b_re, b_im):
    _, g, p, n = b_re.shape
    rows = 2 * g * n
    ex = lambda t: jnp.broadcast_to(t[:, :, None, :], (2, g, n, p)).reshape(rows, p)
    ldt = jnp.broadcast_to(log_dt[:, :, None, None], (2, g, n, p)).reshape(rows, p)
    tb = lambda t: t.transpose(0, 1, 3, 2).reshape(rows, p)
    spec = pl.BlockSpec((rows, p), lambda: (0, 0))
    sds = jax.ShapeDtypeStruct((rows, p), _F32)
    lbr, lbi, bbr, bbi = pl.pallas_call(
        _s5_disc_kernel,
        in_specs=[spec] * 5, out_specs=[spec] * 4, out_shape=[sds] * 4,
        name="s5_discretise",
    )(ex(lam_re), ex(lam_im), ldt, tb(b_re), tb(b_im))
    shp = (2, g, n, p)
    return lbr.reshape(shp)[:, :, 0, :], lbi.reshape(shp)[:, :, 0, :], bbr.reshape(shp), bbi.reshape(shp)


def _s5_scan_kernel(ucf_ref, uxf_ref, ucr_ref, uxr_ref, perm_ref, permt_ref, bf_ref, br_ref, l_re_ref, l_im_ref,
                    cf_re_ref, cf_im_ref, cr_re_ref, cr_im_ref,
                    yfc_ref, yfx_ref, yrc_ref, yrx_ref,
                    d0a, d0b, d0c, d0d, d1a, d1b, d1c, d1d, h0a, h0b, h0c, h0d, h1a, h1b, h1c, h1d,
                    st_re, st_im, *, ts, nb, nc2, nst, sw):
    g = pl.program_id(0)
    d0, d1 = (d0a, d0b, d0c, d0d), (d1a, d1b, d1c, d1d)
    h0, h1 = (h0a, h0b, h0c, h0d), (h1a, h1b, h1c, h1d)

    @pl.when(g == 0)
    def _():
        for r in d0 + d1 + h0 + h1 + (st_re, st_im):
            r[...] = jnp.zeros_like(r)

    ntile = ts // 2
    lower = lax.broadcasted_iota(jnp.int32, (8, nst), 0) < nb
    is_ctx_in = g < nc2
    u_f = jnp.where(is_ctx_in, ucf_ref[...], uxf_ref[...])
    u_r = jnp.where(is_ctx_in, ucr_ref[...], uxr_ref[...])

    def time_batch_rows(u):
        stack = jnp.concatenate([u[:, b * sw:(b + 1) * sw] for b in range(nb)], axis=0)
        return _dot(perm_ref[...], stack).astype(_BF16)

    def drive(uf, ur, d):
        uf, ur = time_batch_rows(uf), time_batch_rows(ur)
        d[0][...] = _dot(uf, bf_ref[:, 0:nst])
        d[1][...] = _dot(uf, bf_ref[:, nst:2 * nst])
        d[2][...] = _dot(ur, br_ref[:, 0:nst])
        d[3][...] = _dot(ur, br_ref[:, nst:2 * nst])

    def scan(d, h):
        l_re, l_im = l_re_ref[...], l_im_ref[...]
        x_re, x_im = st_re[...], st_im[...]

        def step(x_re, x_im, s_re, s_im):
            return l_re * x_re - l_im * x_im + s_re, l_re * x_im + l_im * x_re + s_im

        for k in range(ntile // 2):
            f_tiles, r_tiles = [], []
            for m in (2 * k, 2 * k + 1):
                tf = slice(8 * m, 8 * m + 8)
                tr = slice(8 * (ntile - 1 - m), 8 * (ntile - m))
                a_re, a_im, b_re, b_im = d[0][tf, :], d[1][tf, :], d[2][tr, :], d[3][tr, :]
                e_re, e_im = step(x_re, x_im, jnp.where(lower, a_re, b_re), jnp.where(lower, a_im, b_im))
                x_re, x_im = step(e_re, e_im, pltpu.roll(jnp.where(lower, b_re, a_re), nb, 0),
                                  pltpu.roll(jnp.where(lower, b_im, a_im), nb, 0))
                o_re, o_im = pltpu.roll(x_re, nb, 0), pltpu.roll(x_im, nb, 0)
                f_tiles.append((jnp.where(lower, e_re, o_re), jnp.where(lower, e_im, o_im)))
                r_tiles.append((jnp.where(lower, o_re, e_re), jnp.where(lower, o_im, e_im)))
            pf = slice(16 * k, 16 * k + 16)
            pr = slice(16 * (ntile // 2 - 1 - k), 16 * (ntile // 2 - k))
            pack = lambda lo_t, hi_t: jnp.concatenate([lo_t, hi_t], axis=0).astype(_BF16)
            h[0][pf, :] = pack(f_tiles[0][0], f_tiles[1][0])
            h[1][pf, :] = pack(f_tiles[0][1], f_tiles[1][1])
            h[2][pr, :] = pack(r_tiles[1][0], r_tiles[0][0])
            h[3][pr, :] = pack(r_tiles[1][1], r_tiles[0][1])
        st_re[...] = x_re
        st_im[...] = x_im

    def readout(h):
        yf = (_dot(h[0][...], cf_re_ref[...]) - _dot(h[1][...], cf_im_ref[...])).astype(_BF16)
        yr = (_dot(h[2][...], cr_re_ref[...]) - _dot(h[3][...], cr_im_ref[...])).astype(_BF16)
        return _dot(permt_ref[...], yf).astype(_BF16), _dot(permt_ref[...], yr).astype(_BF16)

    lo, hi = slice(0, ts), slice(ts, 2 * ts)
    scan(d1, h1)
    drive(u_f[lo], u_r[hi], d0)
    yf0, yr0 = readout(h0)
    drive(u_f[hi], u_r[lo], d1)
    scan(d0, h0)
    yf1, yr1 = readout(h1)

    def store(yf_ref, yr_ref):
        for b in range(nb):
            cs = slice(b * sw, (b + 1) * sw)
            rows = slice(b * ts, (b + 1) * ts)
            yf_ref[lo, cs] = yf0[rows]
            yf_ref[hi, cs] = yf1[rows]
            yr_ref[hi, cs] = yr0[rows]
            yr_ref[lo, cs] = yr1[rows]

    pl.when(jnp.logical_and(g >= 1, g <= nc2))(lambda: store(yfc_ref, yrc_ref))
    pl.when(g > nc2)(lambda: store(yfx_ref, yrx_ref))


def _s5_scan(u_c, u_x, bf, br, l_re, l_im, cf_re, cf_im, cr_re, cr_im, nb, ts):
    sw = u_x.shape[1] // nb
    nst = l_re.shape[1]
    blk = 2 * ts
    nc2, nx2 = u_c.shape[0] // blk, u_x.shape[0] // blk
    nt2 = nc2 + nx2
    perm = np.zeros((ts * nb, ts * nb), np.float32)
    for t in range(ts):
        for b in range(nb):
            perm[t * nb + b, b * ts + t] = 1.0
    full = lambda a: pl.BlockSpec(a.shape, lambda g: (0, 0))
    spec = lambda f: pl.BlockSpec((blk, nb * sw), lambda g: (f(g), 0))
    clip = lambda v, n: jnp.clip(v, 0, n - 1)
    ins = [spec(lambda g: clip(g, nc2)), spec(lambda g: clip(g - nc2, nx2)),
           spec(lambda g: clip(nc2 - 1 - g, nc2)), spec(lambda g: clip(nt2 - 1 - g, nx2))]
    outs = [spec(lambda g: clip(g - 1, nc2)), spec(lambda g: clip(g - 1 - nc2, nx2)),
            spec(lambda g: clip(nc2 - g, nc2)), spec(lambda g: clip(nt2 - g, nx2))]
    perm_b, permt_b = _table_bf16(perm), _table_bf16(perm.T)
    sds = lambda a: jax.ShapeDtypeStruct(a.shape, _BF16)
    return pl.pallas_call(
        functools.partial(_s5_scan_kernel, ts=ts, nb=nb, nc2=nc2, nst=nst, sw=sw),
        grid=(nt2 + 1,),
        in_specs=ins + [full(perm_b), full(permt_b), full(bf), full(br), full(l_re), full(l_im),
                        full(cf_re), full(cf_im), full(cr_re), full(cr_im)],
        out_specs=outs,
        out_shape=[sds(u_c), sds(u_x), sds(u_c), sds(u_x)],
        scratch_shapes=([pltpu.VMEM((ts * nb, nst), _F32)] * 8 + [pltpu.VMEM((ts * nb, nst), _BF16)] * 8
                        + [pltpu.VMEM((8, nst), _F32)] * 2),
        compiler_params=_cparams("arbitrary"),
        name="s5_scan",
    )(u_c, u_x, u_c, u_x, perm_b, permt_b, bf, br, l_re, l_im, cf_re, cf_im, cr_re, cr_im)


def _block_diag(t):
    g, r, c = t.shape
    eye = jnp.eye(g, dtype=t.dtype)
    return (t[:, :, None, :] * eye[:, None, :, None]).reshape(g * r, g * c)


def _s5_mix(zs_x, zs_c, nb, lam_re, lam_im, log_dt, b_re, b_im, c_re, c_im):
    assert 2 * nb == 8, "the scan packs batch x direction on the 8 sublanes"
    lbr, lbi, bbr, bbi = _s5_discretise(lam_re, lam_im, log_dt, b_re, b_im)
    nst = lbr.shape[1] * lbr.shape[2]
    drive = lambda d: jnp.concatenate([_block_diag(bbr[d]), _block_diag(bbi[d])], axis=1).astype(_BF16)
    rd = lambda t: _block_diag(t.transpose(0, 2, 1)).astype(_BF16)
    tile = lambda t: jnp.concatenate([jnp.broadcast_to(t[0].reshape(1, nst), (nb, nst)),
                                      jnp.broadcast_to(t[1].reshape(1, nst), (nb, nst))], axis=0)
    yf_c, yf_x, yr_c, yr_x = _s5_scan(zs_c, zs_x, drive(0), drive(1), tile(lbr), tile(lbi),
                                      rd(c_re[0]), rd(c_im[0]), rd(c_re[1]), rd(c_im[1]), nb, S5_TS)
    return (yf_x, yr_x), (yf_c, yr_c)


def _na_row_entries(rows):
    r = np.concatenate([np.arange(NA_QROWS), [NA_QROWS], rows - NA_QROWS + np.arange(NA_QROWS)])
    rs = np.clip(r - NA_KH // 2, 0, rows - NA_KH)
    idx = rs[:, None] + np.arange(NA_KH)[None, :] - r[:, None] + NA_KH - 1
    return tuple(tuple(int(v) for v in row) for row in idx)


def _na_table_kernel(rpb_ref, sel_ref, valid_ref, o_ref):
    t = jnp.dot(rpb_ref[...], sel_ref[...], precision=lax.Precision.HIGHEST, preferred_element_type=_F32)
    o_ref[...] = jnp.where(valid_ref[...] > 0.0, t, NEG_BIG)


def _na_column_tables(rpb):
    h, nr, ncol = rpb.shape
    w = np.arange(GRID_W)
    kc = np.arange(GRID_W)
    cs = np.clip(w - NA_KW // 2, 0, GRID_W - NA_KW)
    valid_c = (kc[None, :] >= cs[:, None]) & (kc[None, :] < cs[:, None] + NA_KW)
    idx_c = np.clip(kc[None, :] - w[:, None] + NA_KW - 1, 0, ncol - 1).reshape(-1)
    kpad = -(-ncol // 8) * 8
    sel = np.zeros((kpad, GRID_W * GRID_W), np.float32)
    sel[idx_c, np.arange(GRID_W * GRID_W)] = 1.0
    full = lambda r, c: pl.BlockSpec((r, c), lambda: (0, 0))
    out = pl.pallas_call(
        _na_table_kernel,
        in_specs=[full(h * nr, kpad), full(kpad, GRID_W * GRID_W), full(1, GRID_W * GRID_W)],
        out_specs=full(h * nr, GRID_W * GRID_W),
        out_shape=jax.ShapeDtypeStruct((h * nr, GRID_W * GRID_W), _F32),
        name="na_tables",
    )(jnp.pad(rpb.reshape(h * nr, ncol), ((0, 0), (0, kpad - ncol))), jnp.asarray(sel),
      jnp.asarray(valid_c.reshape(1, -1).astype(np.float32)))
    return out.reshape(h, nr, GRID_W, GRID_W)


def _na_kernel(q_ref, k_ref, v_ref, kc_ref, vc_ref, t_ref, o_ref, tab_ref, *, rows, nb, row_entries):
    blk = pl.program_id(2)
    dh = NA_HEAD_DIM
    win = NA_KH * GRID_W
    gw = GRID_W

    @pl.when(jnp.logical_and(pl.program_id(1) == 0, blk == 0))
    def _():
        for e, idx in enumerate(row_entries):
            for h in range(2):
                tab_ref[e, h * gw:(h + 1) * gw, :] = jnp.concatenate([t_ref[h, a] for a in idx], axis=1)

    q = q_ref[...]
    head0 = lax.broadcasted_iota(jnp.int32, q.shape, 1) < dh
    zero = jnp.zeros_like(q)
    q0, q1 = jnp.where(head0, q, zero), jnp.where(head0, zero, q)
    qq = jnp.concatenate([t[dr * gw:(dr + 1) * gw] for dr in range(NA_QROWS) for t in (q0, q1)], axis=0)
    starts, entries = [], []
    for dr in range(NA_QROWS):
        r = blk * NA_QROWS + dr
        rs = jnp.clip(r - NA_KH // 2, 0, rows - NA_KH)
        starts.append(pl.multiple_of(rs * gw, gw))
        entries.append(jnp.where(blk == 0, dr, jnp.where(blk == nb - 1, NA_QROWS + 1 + dr, NA_QROWS)))
    pair = lambda dr: slice(2 * dr * gw, 2 * (dr + 1) * gw)
    s_ctx = _dot_nt(qq, kc_ref[...])
    s_loc = jnp.concatenate(
        [_dot_nt(qq[pair(dr)], k_ref[pl.ds(starts[dr], win), :]) + tab_ref[entries[dr]]
         for dr in range(NA_QROWS)], axis=0)
    m = jnp.maximum(jnp.max(s_loc, axis=-1, keepdims=True), jnp.max(s_ctx, axis=-1, keepdims=True))
    p_loc = jnp.exp(s_loc - m)
    p_ctx = jnp.exp(s_ctx - m)
    den = jnp.sum(p_loc, axis=-1, keepdims=True) + jnp.sum(p_ctx, axis=-1, keepdims=True)
    p_loc = p_loc.astype(_BF16)
    o = jnp.concatenate(
        [_dot(p_loc[pair(dr)], v_ref[pl.ds(starts[dr], win), :]) for dr in range(NA_QROWS)], axis=0)
    o = (o + _dot(p_ctx.astype(_BF16), vc_ref[...])) / den
    h0_lanes = lax.broadcasted_iota(jnp.int32, (gw, q.shape[1]), 1) < dh
    for dr in range(NA_QROWS):
        o_ref[dr * gw:(dr + 1) * gw, :] = jnp.where(
            h0_lanes, o[2 * dr * gw:(2 * dr + 1) * gw], o[(2 * dr + 1) * gw:(2 * dr + 2) * gw]).astype(_BF16)


def _na_latent(q, k, v, kc, vc, rpb):
    b, l, aw = q.shape
    lc = kc.shape[1]
    rows = l // GRID_W
    assert rows % NA_QROWS == 0 and rows >= 2 * NA_QROWS
    nb = rows // NA_QROWS
    hp = aw // (2 * NA_HEAD_DIM)
    tq = NA_QROWS * GRID_W
    row_entries = _na_row_entries(rows)
    tables = _na_column_tables(rpb.astype(_F32))
    tables = tables.reshape((hp, 2) + tables.shape[1:])
    return pl.pallas_call(
        functools.partial(_na_kernel, rows=rows, nb=nb, row_entries=row_entries),
        grid=(hp, b, nb),
        in_specs=[pl.BlockSpec((None, tq, 2 * NA_HEAD_DIM), lambda p, bi, j: (bi, j, p)),
                  pl.BlockSpec((None, l, 2 * NA_HEAD_DIM), lambda p, bi, j: (bi, 0, p)),
                  pl.BlockSpec((None, l, 2 * NA_HEAD_DIM), lambda p, bi, j: (bi, 0, p)),
                  pl.BlockSpec((None, lc, 2 * NA_HEAD_DIM), lambda p, bi, j: (bi, 0, p)),
                  pl.BlockSpec((None, lc, 2 * NA_HEAD_DIM), lambda p, bi, j: (bi, 0, p)),
                  pl.BlockSpec((None,) + tables.shape[1:], lambda p, bi, j: (p, 0, 0, 0, 0))],
        out_specs=pl.BlockSpec((None, tq, 2 * NA_HEAD_DIM), lambda p, bi, j: (bi, j, p)),
        out_shape=jax.ShapeDtypeStruct((b, l, aw), _BF16),
        scratch_shapes=[pltpu.VMEM((len(row_entries), 2 * GRID_W, NA_KH * GRID_W), _F32)],
        compiler_params=_cparams("arbitrary", "arbitrary", "arbitrary"),
        name="na_latent",
    )(q, k, v, kc, vc, tables)


def _ctx_attn_kernel(q_ref, k_ref, v_ref, o_ref):
    dh = NA_HEAD_DIM
    outs = []
    for h in range(2):
        ls = slice(h * dh, (h + 1) * dh)
        s = _dot_nt(q_ref[:, ls], k_ref[:, ls])
        m = jnp.max(s, axis=-1, keepdims=True)
        p = jnp.exp(s - m)
        den = jnp.sum(p, axis=-1, keepdims=True)
        outs.append(_dot(p.astype(_BF16), v_ref[:, ls]) / den)
    o_ref[...] = jnp.concatenate(outs, axis=-1).astype(_BF16)


def _ctx_attention(q, k, v):
    b, lc, aw = q.shape
    hp = aw // (2 * NA_HEAD_DIM)
    spec = pl.BlockSpec((None, lc, 2 * NA_HEAD_DIM), lambda bi, p: (bi, 0, p))
    return pl.pallas_call(
        _ctx_attn_kernel,
        grid=(b, hp),
        in_specs=[spec] * 3, out_specs=spec,
        out_shape=jax.ShapeDtypeStruct((b, lc, aw), _BF16),
        compiler_params=_cparams("parallel", "parallel"),
        name="ctx_attention",
    )(q, k, v)


def _gelu_tanh(x):
    return 0.5 * x * (1.0 + jnp.tanh(math.sqrt(2.0 / math.pi) * (x + 0.044715 * (x * x * x))))


def _merge_kernel(x_ref, mod_ref, g_ref, fa_ref, u_ref, yf_ref, yr_ref, nc_ref, d_ref, wglu_ref,
                  wg_ref, wa_ref, wb_ref, wc_ref, wo_ref, o_ref):
    x = x_ref[...]
    d = x.shape[1]
    h = _norm_mod(x, g_ref[...], mod_ref[0:1, :], mod_ref[1:2, :]).astype(_BF16)
    y = u_ref[...].astype(_F32) * d_ref[...] + yf_ref[...].astype(_F32) + yr_ref[...].astype(_F32)
    gl = _gelu_tanh(y)
    sb = gl * jax.nn.sigmoid(_dot(gl.astype(_BF16), wglu_ref[...]))
    m = jax.nn.sigmoid(_dot(h, wg_ref[:, 0:d])) * _dot(fa_ref[...], wa_ref[...])
    m += jax.nn.sigmoid(_dot(h, wg_ref[:, d:2 * d])) * _dot(sb.astype(_BF16), wb_ref[...])
    m += jax.nn.sigmoid(_dot(h, wg_ref[:, 2 * d:3 * d])) * _dot(nc_ref[...], wc_ref[...])
    o_ref[...] = x + mod_ref[2:3, :] * _dot(m.astype(_BF16), wo_ref[...])


def _merge(x, mod, g, fa, u, yf, yr, nc, d_skip, w_glu, w_gate, w_a, w_b, w_c, w_out, tm):
    b, l, d = x.shape
    tok = lambda a: pl.BlockSpec((None, tm, a.shape[2]), lambda bi, i: (bi, i, 0))
    tmaj = lambda a: pl.BlockSpec((tm, a.shape[1] // b), lambda bi, i: (i, bi))
    full = lambda a: pl.BlockSpec(a.shape, lambda bi, i: (0, 0))
    return pl.pallas_call(
        _merge_kernel,
        grid=(b, l // tm),
        in_specs=[tok(x), pl.BlockSpec((None, 6, d), lambda bi, i: (bi, 0, 0)), full(g),
                  tok(fa), tmaj(u), tmaj(yf), tmaj(yr), tok(nc), full(d_skip), full(w_glu),
                  full(w_gate), full(w_a), full(w_b), full(w_c), full(w_out)],
        out_specs=tok(x),
        out_shape=jax.ShapeDtypeStruct((b, l, d), _F32),
        compiler_params=_cparams("parallel", "parallel"),
        name="merge",
    )(x, mod, g, fa, u, yf, yr, nc, d_skip, w_glu, w_gate, w_a, w_b, w_c, w_out)


def _ffn_kernel(x_ref, mod_ref, g_ref, w1_ref, w2_ref, gf_ref, o_ref, *, final_norm, fchunk):
    x = x_ref[...]
    h = _norm_mod(x, g_ref[...], mod_ref[3:4, :], mod_ref[4:5, :]).astype(_BF16)
    dff = w1_ref.shape[1]
    acc = jnp.zeros_like(x)
    for c in range(dff // fchunk):
        cs = slice(c * fchunk, (c + 1) * fchunk)
        t = jnp.maximum(_dot(h, w1_ref[:, cs]), 0.0)
        acc += _dot((t * t).astype(_BF16), w2_ref[cs, :])
    x2 = x + mod_ref[5:6, :] * acc
    if final_norm:
        ms = jnp.mean(x2 * x2, axis=-1, keepdims=True)
        x2 = x2 * lax.rsqrt(ms + RMS_EPS) * gf_ref[...]
    o_ref[...] = x2


def _ffn(x, mod, g, w1, w2, g_final, final_norm, tm):
    b, l, d = x.shape
    tok = pl.BlockSpec((None, tm, d), lambda bi, i: (bi, i, 0))
    full = lambda a: pl.BlockSpec(a.shape, lambda bi, i: (0, 0))
    return pl.pallas_call(
        functools.partial(_ffn_kernel, final_norm=final_norm, fchunk=1024),
        grid=(b, l // tm),
        in_specs=[tok, pl.BlockSpec((None, 6, d), lambda bi, i: (bi, 0, 0)), full(g),
                  full(w1), full(w2), full(g_final)],
        out_specs=tok,
        out_shape=jax.ShapeDtypeStruct((b, l, d), _F32),
        compiler_params=_cparams("parallel", "parallel"),
        name="ffn",
    )(x, mod, g, w1, w2, g_final)


def kernel(x, c, ctx, c_ctx, w_mod, b_mod, g_norm1, g_norm2, w_in, w_br_a, w_br_b, w_br_c, w_out, s5_lam_re, s5_lam_im, s5_log_dt, s5_b_re, s5_b_im, s5_c_re, s5_c_im, s5_d, s5_w_glu, na_rpb, w_ff1, w_ff2, g_final):
    b, l, d = x.shape
    lc = ctx.shape[1]
    depth = w_mod.shape[0]
    fw = w_br_a.shape[1]
    sw = w_br_b.shape[1]
    aw = w_br_c.shape[1]
    off_g = fw + sw + 3 * aw
    tm = 512
    tmc = lc

    c8 = jnp.concatenate([c, c_ctx[None, :], jnp.zeros((8 - b - 1, d), _F32)], axis=0)
    mod = _modulation(c8, w_mod, b_mod).reshape(depth, 8, 6, d)
    gf = g_final.reshape(1, d)

    for i in range(depth):
        need_ctx_out = i < depth - 1
        mod_x = mod[i, :b]
        mod_c = jnp.broadcast_to(mod[i, b][None], (b, 6, d))
        g1 = g_norm1[i].reshape(1, d)
        g2 = g_norm2[i].reshape(1, d)
        w_proj = w_in[i, :, :off_g].astype(_BF16)
        w_gate = w_in[i, :, off_g:].astype(_BF16)
        widths = (fw, sw, aw)

        zf_x, zs_x, q_x, k_x, v_x = _inproj(x, mod_x, g1, w_proj, widths, tm)
        zf_c, zs_c, q_c, k_c, v_c = _inproj(ctx, mod_c, g1, w_proj, widths, tmc)

        fa_x = _fnet_latent(zf_x)
        (yf_x, yr_x), (yf_c, yr_c) = _s5_mix(zs_x, zs_c, b, s5_lam_re[i], s5_lam_im[i], s5_log_dt[i],
                                              s5_b_re[i], s5_b_im[i], s5_c_re[i], s5_c_im[i])
        nc_x = _na_latent(q_x, k_x, v_x, k_c, v_c, na_rpb[i])

        branch_w = (s5_d[i].reshape(1, sw), s5_w_glu[i].astype(_BF16), w_gate, w_br_a[i].astype(_BF16),
                    w_br_b[i].astype(_BF16), w_br_c[i].astype(_BF16), w_out[i].astype(_BF16))
        w1 = w_ff1[i].astype(_BF16)
        w2 = w_ff2[i].astype(_BF16)

        x = _merge(x, mod_x, g1, fa_x, zs_x, yf_x, yr_x, nc_x, *branch_w, tm)
        x = _ffn(x, mod_x, g2, w1, w2, gf, not need_ctx_out, tm)

        if need_ctx_out:
            fa_c = _fnet_ctx(zf_c)
            nc_c = _ctx_attention(q_c, k_c, v_c)
            ctx = _merge(ctx, mod_c, g1, fa_c, zs_c, yf_c, yr_c, nc_c, *branch_w, tmc)
            ctx = _ffn(ctx, mod_c, g2, w1, w2, gf, False, tmc)
    return x
```

```python
import functools
import math

import numpy as np
import jax
import jax.numpy as jnp
from jax import lax
from jax.experimental import pallas as pl
from jax.experimental.pallas import tpu as pltpu

_F32 = jnp.float32
_BF16 = jnp.bfloat16

RMS_EPS = 1e-6
GRID_W = 64
F_GROUP_DIM = 64
S5_GROUP_DIM = 16
NA_HEAD_DIM = 64
NA_KH = 8
NA_KW = 16
NA_QROWS = 8
NEG_BIG = -1e30
FFT_L2 = 128
S5_TS = 64

VMEM_LIMIT_BYTES = 56 * 1024 * 1024


def _cparams(*sem):
    return pltpu.CompilerParams(dimension_semantics=sem, vmem_limit_bytes=VMEM_LIMIT_BYTES)


def _dot(a, b):
    return jnp.dot(a, b, preferred_element_type=_F32)


def _table_bf16(t):
    return jnp.asarray(t, _F32).astype(_BF16)


def _dot_nt(a, b):
    return lax.dot_general(a, b, (((1,), (1,)), ((), ())), preferred_element_type=_F32)


def _norm_mod(x, g, shift, scale):
    ms = jnp.mean(x * x, axis=-1, keepdims=True)
    y = x * lax.rsqrt(ms + RMS_EPS) * g
    return y * (1.0 + scale) + shift


def _mod_kernel(c_ref, w_ref, b_ref, o_ref):
    c = c_ref[...]
    s = c * jax.nn.sigmoid(c)
    o_ref[...] = jnp.dot(s, w_ref[...], precision=lax.Precision.HIGHEST,
                         preferred_element_type=_F32) + b_ref[...]


def _modulation(c8, w_mod, b_mod):
    depth, d, n = w_mod.shape
    tn = 1024
    return pl.pallas_call(
        _mod_kernel,
        grid=(depth, n // tn),
        in_specs=[pl.BlockSpec((8, d), lambda l, j: (0, 0)),
                  pl.BlockSpec((None, d, tn), lambda l, j: (l, 0, j)),
                  pl.BlockSpec((None, 1, tn), lambda l, j: (l, 0, j))],
        out_specs=pl.BlockSpec((None, 8, tn), lambda l, j: (l, 0, j)),
        out_shape=jax.ShapeDtypeStruct((depth, 8, n), _F32),
        compiler_params=_cparams("parallel", "parallel"),
        name="modulation",
    )(c8, w_mod, b_mod.reshape(depth, 1, n))


def _inproj_kernel(x_ref, mod_ref, g_ref, w_ref, zf_ref, zs_ref, q_ref, k_ref, v_ref, *, widths, q_scale):
    h = _norm_mod(x_ref[...], g_ref[...], mod_ref[0:1, :], mod_ref[1:2, :]).astype(_BF16)
    fw, sw, aw = widths
    o = 0
    zf_ref[...] = _dot(h, w_ref[:, o:o + fw]).astype(_BF16)
    o += fw
    zs_ref[...] = _dot(h, w_ref[:, o:o + sw]).astype(_BF16)
    o += sw
    q_ref[...] = (_dot(h, w_ref[:, o:o + aw]) * q_scale).astype(_BF16)
    o += aw
    k_ref[...] = _dot(h, w_ref[:, o:o + aw]).astype(_BF16)
    o += aw
    v_ref[...] = _dot(h, w_ref[:, o:o + aw]).astype(_BF16)


def _layer_specs(layer, mod_row, d):
    mod = pl.BlockSpec((None, None, 6, d), lambda bi, i: (layer, bi if mod_row is None else mod_row, 0, 0))
    mat = lambda a: pl.BlockSpec((None,) + a.shape[1:], lambda bi, i: (layer, 0, 0))
    return mod, mat


def _inproj(x, mod, g, w, layer, mod_row, widths, tm):
    b, l, d = x.shape
    fw, sw, aw = widths
    n = fw + sw + 3 * aw
    mod_spec, mat = _layer_specs(layer, mod_row, d)
    tok = lambda width: pl.BlockSpec((None, tm, width), lambda bi, i: (bi, i, 0))
    return pl.pallas_call(
        functools.partial(_inproj_kernel, widths=widths, q_scale=NA_HEAD_DIM ** -0.5),
        grid=(b, l // tm),
        in_specs=[tok(d), mod_spec, mat(g), pl.BlockSpec((None, d, n), lambda bi, i: (layer, 0, 0))],
        out_specs=[tok(fw), pl.BlockSpec((tm, sw), lambda bi, i: (i, bi)), tok(aw), tok(aw), tok(aw)],
        out_shape=[jax.ShapeDtypeStruct((b, l, fw), _BF16), jax.ShapeDtypeStruct((l, b * sw), _BF16),
                   jax.ShapeDtypeStruct((b, l, aw), _BF16), jax.ShapeDtypeStruct((b, l, aw), _BF16),
                   jax.ShapeDtypeStruct((b, l, aw), _BF16)],
        compiler_params=_cparams("parallel", "parallel"),
        name="inproj",
    )(x, mod, g, w)


def _dft_tables(l):
    l1, l2 = l // FFT_L2, FFT_L2
    a1 = 2 * np.pi * np.outer(np.arange(l1), np.arange(l1)) / l1
    f1 = np.concatenate([np.cos(a1), np.sin(a1)], axis=0)
    at = 2 * np.pi * np.outer(np.arange(l1), np.arange(l2)) / l
    a2 = 2 * np.pi * np.outer(np.arange(l2), np.arange(l2)) / l2
    c2, s2 = np.cos(a2), np.sin(a2)
    f2 = np.block([[c2, -s2], [s2, c2]])
    return f1, np.cos(at), np.sin(at), f2


def _group_dft_tables(width):
    ac = 2 * np.pi * np.outer(np.arange(F_GROUP_DIM), np.arange(F_GROUP_DIM)) / F_GROUP_DIM
    eye = np.eye(width // F_GROUP_DIM)
    return np.kron(eye, np.cos(ac)), np.kron(eye, np.sin(ac))


def _fnet_a_kernel(x_ref, f1_ref, cw_ref, sw_ref, o_ref):
    a = _dot(f1_ref[...], x_ref[...])
    l1 = a.shape[0] // 2
    ac, asn = a[:l1], a[l1:]
    cw, sw = cw_ref[...], sw_ref[...]
    o_ref[0] = (ac * cw - asn * sw).astype(_BF16)
    o_ref[1] = (ac * sw + asn * cw).astype(_BF16)


def _fnet_b_kernel(a_ref, f2_ref, bc_ref, bs_ref, o_ref, *, scale, nk1, width):
    half = f2_ref.shape[0] // 2
    for j in range(nk1):
        rhs = jnp.concatenate([a_ref[0, j], a_ref[1, j]], axis=0)
        r = _dot(f2_ref[...], rhs)
        out = _dot(r[:half].astype(_BF16), bc_ref[...]) - _dot(r[half:].astype(_BF16), bs_ref[...])
        o_ref[:, j * width:(j + 1) * width] = (out * scale).astype(_BF16)


def _fnet_latent(zf):
    b, l, width = zf.shape
    l1, l2 = l // FFT_L2, FFT_L2
    f1, cw, sw, f2 = _dft_tables(l)
    bc, bs = _group_dft_tables(width)
    nc = 16 * width
    cw = jnp.repeat(jnp.asarray(cw, _F32), width, axis=1)
    sw = jnp.repeat(jnp.asarray(sw, _F32), width, axis=1)
    a = pl.pallas_call(
        _fnet_a_kernel,
        grid=(l2 * width // nc, b),
        in_specs=[pl.BlockSpec((None, l1, nc), lambda j, bi: (bi, 0, j)),
                  pl.BlockSpec((2 * l1, l1), lambda j, bi: (0, 0)),
                  pl.BlockSpec((l1, nc), lambda j, bi: (0, j)),
                  pl.BlockSpec((l1, nc), lambda j, bi: (0, j))],
        out_specs=pl.BlockSpec((None, 2, l1, nc), lambda j, bi: (bi, 0, 0, j)),
        out_shape=jax.ShapeDtypeStruct((b, 2, l1, l2 * width), _BF16),
        compiler_params=_cparams("parallel", "parallel"),
        name="fnet_a",
    )(zf.reshape(b, l1, l2 * width), _table_bf16(f1), cw, sw)
    nk1 = 8
    out = pl.pallas_call(
        functools.partial(_fnet_b_kernel, scale=float((l * F_GROUP_DIM) ** -0.5), nk1=nk1, width=width),
        grid=(b, l1 // nk1),
        in_specs=[pl.BlockSpec((None, 2, nk1, l2, width), lambda bi, j: (bi, 0, j, 0, 0)),
                  pl.BlockSpec((2 * l2, 2 * l2), lambda bi, j: (0, 0)),
                  pl.BlockSpec((width, width), lambda bi, j: (0, 0)),
                  pl.BlockSpec((width, width), lambda bi, j: (0, 0))],
        out_specs=pl.BlockSpec((None, l2, nk1 * width), lambda bi, j: (bi, 0, j)),
        out_shape=jax.ShapeDtypeStruct((b, l2, l1 * width), _BF16),
        compiler_params=_cparams("parallel", "parallel"),
        name="fnet_b",
    )(a.reshape(b, 2, l1, l2, width), _table_bf16(f2), _table_bf16(bc), _table_bf16(bs))
    return out.reshape(b, l, width)


def _fnet_ctx_kernel(z_ref, cl_ref, sl_ref, bc_ref, bs_ref, o_ref, *, scale):
    z = z_ref[...]
    zc = _dot(z, bc_ref[...]).astype(_BF16)
    zs = _dot(z, bs_ref[...]).astype(_BF16)
    o_ref[...] = ((_dot(cl_ref[...], zc) - _dot(sl_ref[...], zs)) * scale).astype(_BF16)


def _fnet_ctx(zf):
    b, l, width = zf.shape
    al = 2 * np.pi * np.outer(np.arange(l), np.arange(l)) / l
    bc, bs = _group_dft_tables(width)
    full = lambda r, c: pl.BlockSpec((r, c), lambda bi: (0, 0))
    return pl.pallas_call(
        functools.partial(_fnet_ctx_kernel, scale=float((l * F_GROUP_DIM) ** -0.5)),
        grid=(b,),
        in_specs=[pl.BlockSpec((None, l, width), lambda bi: (bi, 0, 0)),
                  full(l, l), full(l, l), full(width, width), full(width, width)],
        out_specs=pl.BlockSpec((None, l, width), lambda bi: (bi, 0, 0)),
        out_shape=jax.ShapeDtypeStruct((b, l, width), _BF16),
        compiler_params=_cparams("parallel"),
        name="fnet_ctx",
    )(zf, _table_bf16(np.cos(al)), _table_bf16(np.sin(al)), _table_bf16(bc), _table_bf16(bs))


def _s5_disc_kernel(lr_ref, li_ref, ldt_ref, br_ref, bi_ref, lbr_ref, lbi_ref, bbr_ref, bbi_ref):
    lr = jnp.minimum(lr_ref[...], -1e-4)
    li = li_ref[...]
    dt = jnp.exp(ldt_ref[...])
    mag = jnp.exp(lr * dt)
    lbr = mag * jnp.cos(li * dt)
    lbi = mag * jnp.sin(li * dt)
    lbr_ref[...] = lbr
    lbi_ref[...] = lbi
    a, bb = lbr - 1.0, lbi
    den = lr * lr + li * li
    cr = (a * lr + bb * li) / den
    ci = (bb * lr - a * li) / den
    br, bi = br_ref[...], bi_ref[...]
    bbr_ref[...] = cr * br - ci * bi
    bbi_ref[...] = cr * bi + ci * br


def _s5_discretise(lam_re, lam_im, log_dt, b_re, b_im):
    _, g, p, n = b_re.shape
    rows = 2 * g * n
    ex = lambda t: jnp.broadcast_to(t[:, :, None, :], (2, g, n, p)).reshape(rows, p)
    ldt = jnp.broadcast_to(log_dt[:, :, None, None], (2, g, n, p)).reshape(rows, p)
    tb = lambda t: t.transpose(0, 1, 3, 2).reshape(rows, p)
    spec = pl.BlockSpec((rows, p), lambda: (0, 0))
    sds = jax.ShapeDtypeStruct((rows, p), _F32)
    lbr, lbi, bbr, bbi = pl.pallas_call(
        _s5_disc_kernel,
        in_specs=[spec] * 5, out_specs=[spec] * 4, out_shape=[sds] * 4,
        name="s5_discretise",
    )(ex(lam_re), ex(lam_im), ldt, tb(b_re), tb(b_im))
    shp = (2, g, n, p)
    return lbr.reshape(shp)[:, :, 0, :], lbi.reshape(shp)[:, :, 0, :], bbr.reshape(shp), bbi.reshape(shp)


def _s5_scan_kernel(ucf_ref, uxf_ref, ucr_ref, uxr_ref, perm_ref, permt_ref, bf_ref, br_ref, l_re_ref, l_im_ref,
                    cf_re_ref, cf_im_ref, cr_re_ref, cr_im_ref,
                    yfc_ref, yfx_ref, yrc_ref, yrx_ref,
                    d0a, d0b, d0c, d0d, d1a, d1b, d1c, d1d, h0a, h0b, h0c, h0d, h1a, h1b, h1c, h1d,
                    st_re, st_im, *, ts, nb, nc2, nst, sw):
    g = pl.program_id(0)
    d0, d1 = (d0a, d0b, d0c, d0d), (d1a, d1b, d1c, d1d)
    h0, h1 = (h0a, h0b, h0c, h0d), (h1a, h1b, h1c, h1d)

    @pl.when(g == 0)
    def _():
        for r in d0 + d1 + h0 + h1 + (st_re, st_im):
            r[...] = jnp.zeros_like(r)

    ntile = ts // 2
    lower = lax.broadcasted_iota(jnp.int32, (8, nst), 0) < nb
    is_ctx_in = g < nc2
    u_f = jnp.where(is_ctx_in, ucf_ref[...], uxf_ref[...])
    u_r = jnp.where(is_ctx_in, ucr_ref[...], uxr_ref[...])

    def time_batch_rows(u):
        stack = jnp.concatenate([u[:, b * sw:(b + 1) * sw] for b in range(nb)], axis=0)
        return _dot(perm_ref[...], stack).astype(_BF16)

    def drive(uf, ur, d):
        uf, ur = time_batch_rows(uf), time_batch_rows(ur)
        d[0][...] = _dot(uf, bf_ref[:, 0:nst])
        d[1][...] = _dot(uf, bf_ref[:, nst:2 * nst])
        d[2][...] = _dot(ur, br_ref[:, 0:nst])
        d[3][...] = _dot(ur, br_ref[:, nst:2 * nst])

    def scan(d, h):
        l_re, l_im = l_re_ref[...], l_im_ref[...]
        x_re, x_im = st_re[...], st_im[...]

        def step(x_re, x_im, s_re, s_im):
            return l_re * x_re - l_im * x_im + s_re, l_re * x_im + l_im * x_re + s_im

        for k in range(ntile // 2):
            f_tiles, r_tiles = [], []
            for m in (2 * k, 2 * k + 1):
                tf = slice(8 * m, 8 * m + 8)
                tr = slice(8 * (ntile - 1 - m), 8 * (ntile - m))
                a_re, a_im, b_re, b_im = d[0][tf, :], d[1][tf, :], d[2][tr, :], d[3][tr, :]
                e_re, e_im = step(x_re, x_im, jnp.where(lower, a_re, b_re), jnp.where(lower, a_im, b_im))
                x_re, x_im = step(e_re, e_im, pltpu.roll(jnp.where(lower, b_re, a_re), nb, 0),
                                  pltpu.roll(jnp.where(lower, b_im, a_im), nb, 0))
                o_re, o_im = pltpu.roll(x_re, nb, 0), pltpu.roll(x_im, nb, 0)
                f_tiles.append((jnp.where(lower, e_re, o_re), jnp.where(lower, e_im, o_im)))
                r_tiles.append((jnp.where(lower, o_re, e_re), jnp.where(lower, o_im, e_im)))
            pf = slice(16 * k, 16 * k + 16)
            pr = slice(16 * (ntile // 2 - 1 - k), 16 * (ntile // 2 - k))
            pack = lambda lo_t, hi_t: jnp.concatenate([lo_t, hi_t], axis=0).astype(_BF16)
            h[0][pf, :] = pack(f_tiles[0][0], f_tiles[1][0])
            h[1][pf, :] = pack(f_tiles[0][1], f_tiles[1][1])
            h[2][pr, :] = pack(r_tiles[1][0], r_tiles[0][0])
            h[3][pr, :] = pack(r_tiles[1][1], r_tiles[0][1])
        st_re[...] = x_re
        st_im[...] = x_im

    def readout(h):
        yf = (_dot(h[0][...], cf_re_ref[...]) - _dot(h[1][...], cf_im_ref[...])).astype(_BF16)
        yr = (_dot(h[2][...], cr_re_ref[...]) - _dot(h[3][...], cr_im_ref[...])).astype(_BF16)
        return _dot(permt_ref[...], yf).astype(_BF16), _dot(permt_ref[...], yr).astype(_BF16)

    lo, hi = slice(0, ts), slice(ts, 2 * ts)
    scan(d1, h1)
    drive(u_f[lo], u_r[hi], d0)
    yf0, yr0 = readout(h0)
    drive(u_f[hi], u_r[lo], d1)
    scan(d0, h0)
    yf1, yr1 = readout(h1)

    def store(yf_ref, yr_ref):
        for b in range(nb):
            cs = slice(b * sw, (b + 1) * sw)
            rows = slice(b * ts, (b + 1) * ts)
            yf_ref[lo, cs] = yf0[rows]
            yf_ref[hi, cs] = yf1[rows]
            yr_ref[hi, cs] = yr0[rows]
            yr_ref[lo, cs] = yr1[rows]

    pl.when(jnp.logical_and(g >= 1, g <= nc2))(lambda: store(yfc_ref, yrc_ref))
    pl.when(g > nc2)(lambda: store(yfx_ref, yrx_ref))


def _s5_scan(u_c, u_x, bf, br, l_re, l_im, cf_re, cf_im, cr_re, cr_im, nb, ts):
    sw = u_x.shape[1] // nb
    nst = l_re.shape[1]
    blk = 2 * ts
    nc2, nx2 = u_c.shape[0] // blk, u_x.shape[0] // blk
    nt2 = nc2 + nx2
    perm = np.zeros((ts * nb, ts * nb), np.float32)
    for t in range(ts):
        for b in range(nb):
            perm[t * nb + b, b * ts + t] = 1.0
    full = lambda a: pl.BlockSpec(a.shape, lambda g: (0, 0))
    spec = lambda f: pl.BlockSpec((blk, nb * sw), lambda g: (f(g), 0))
    clip = lambda v, n: jnp.clip(v, 0, n - 1)
    ins = [spec(lambda g: clip(g, nc2)), spec(lambda g: clip(g - nc2, nx2)),
           spec(lambda g: clip(nc2 - 1 - g, nc2)), spec(lambda g: clip(nt2 - 1 - g, nx2))]
    outs = [spec(lambda g: clip(g - 1, nc2)), spec(lambda g: clip(g - 1 - nc2, nx2)),
            spec(lambda g: clip(nc2 - g, nc2)), spec(lambda g: clip(nt2 - g, nx2))]
    perm_b, permt_b = _table_bf16(perm), _table_bf16(perm.T)
    sds = lambda a: jax.ShapeDtypeStruct(a.shape, _BF16)
    return pl.pallas_call(
        functools.partial(_s5_scan_kernel, ts=ts, nb=nb, nc2=nc2, nst=nst, sw=sw),
        grid=(nt2 + 1,),
        in_specs=ins + [full(perm_b), full(permt_b), full(bf), full(br), full(l_re), full(l_im),
                        full(cf_re), full(cf_im), full(cr_re), full(cr_im)],
        out_specs=outs,
        out_shape=[sds(u_c), sds(u_x), sds(u_c), sds(u_x)],
        scratch_shapes=([pltpu.VMEM((ts * nb, nst), _F32)] * 8 + [pltpu.VMEM((ts * nb, nst), _BF16)] * 8
                        + [pltpu.VMEM((8, nst), _F32)] * 2),
        compiler_params=_cparams("arbitrary"),
        name="s5_scan",
    )(u_c, u_x, u_c, u_x, perm_b, permt_b, bf, br, l_re, l_im, cf_re, cf_im, cr_re, cr_im)


def _block_diag(t):
    g, r, c = t.shape
    eye = jnp.eye(g, dtype=t.dtype)
    return (t[:, :, None, :] * eye[:, None, :, None]).reshape(g * r, g * c)


def _s5_mix(zs_x, zs_c, nb, lam_re, lam_im, log_dt, b_re, b_im, c_re, c_im):
    assert 2 * nb == 8, "the scan packs batch x direction on the 8 sublanes"
    lbr, lbi, bbr, bbi = _s5_discretise(lam_re, lam_im, log_dt, b_re, b_im)
    nst = lbr.shape[1] * lbr.shape[2]
    drive = lambda d: jnp.concatenate([_block_diag(bbr[d]), _block_diag(bbi[d])], axis=1).astype(_BF16)
    rd = lambda t: _block_diag(t.transpose(0, 2, 1)).astype(_BF16)
    tile = lambda t: jnp.concatenate([jnp.broadcast_to(t[0].reshape(1, nst), (nb, nst)),
                                      jnp.broadcast_to(t[1].reshape(1, nst), (nb, nst))], axis=0)
    yf_c, yf_x, yr_c, yr_x = _s5_scan(zs_c, zs_x, drive(0), drive(1), tile(lbr), tile(lbi),
                                      rd(c_re[0]), rd(c_im[0]), rd(c_re[1]), rd(c_im[1]), nb, S5_TS)
    return (yf_x, yr_x), (yf_c, yr_c)


def _na_row_entries(rows):
    r = np.concatenate([np.arange(NA_QROWS), [NA_QROWS], rows - NA_QROWS + np.arange(NA_QROWS)])
    rs = np.clip(r - NA_KH // 2, 0, rows - NA_KH)
    idx = rs[:, None] + np.arange(NA_KH)[None, :] - r[:, None] + NA_KH - 1
    return tuple(tuple(int(v) for v in row) for row in idx)


def _na_table_kernel(rpb_ref, sel_ref, valid_ref, o_ref):
    t = jnp.dot(rpb_ref[...], sel_ref[...], precision=lax.Precision.HIGHEST, preferred_element_type=_F32)
    o_ref[...] = jnp.where(valid_ref[...] > 0.0, t, NEG_BIG)


def _na_column_tables(rpb):
    h, nr, ncol = rpb.shape
    w = np.arange(GRID_W)
    kc = np.arange(GRID_W)
    cs = np.clip(w - NA_KW // 2, 0, GRID_W - NA_KW)
    valid_c = (kc[None, :] >= cs[:, None]) & (kc[None, :] < cs[:, None] + NA_KW)
    idx_c = np.clip(kc[None, :] - w[:, None] + NA_KW - 1, 0, ncol - 1).reshape(-1)
    kpad = -(-ncol // 8) * 8
    sel = np.zeros((kpad, GRID_W * GRID_W), np.float32)
    sel[idx_c, np.arange(GRID_W * GRID_W)] = 1.0
    full = lambda r, c: pl.BlockSpec((r, c), lambda: (0, 0))
    out = pl.pallas_call(
        _na_table_kernel,
        in_specs=[full(h * nr, kpad), full(kpad, GRID_W * GRID_W), full(1, GRID_W * GRID_W)],
        out_specs=full(h * nr, GRID_W * GRID_W),
        out_shape=jax.ShapeDtypeStruct((h * nr, GRID_W * GRID_W), _F32),
        name="na_tables",
    )(jnp.pad(rpb.reshape(h * nr, ncol), ((0, 0), (0, kpad - ncol))), jnp.asarray(sel),
      jnp.asarray(valid_c.reshape(1, -1).astype(np.float32)))
    return out.reshape(h, nr, GRID_W, GRID_W)


def _na_kernel(q_ref, k_ref, v_ref, kc_ref, vc_ref, t_ref, o_ref, tab_ref, *, rows, nb, row_entries):
    blk = pl.program_id(2)
    dh = NA_HEAD_DIM
    win = NA_KH * GRID_W
    gw = GRID_W

    @pl.when(jnp.logical_and(pl.program_id(1) == 0, blk == 0))
    def _():
        for e, idx in enumerate(row_entries):
            for h in range(2):
                tab_ref[e, h * gw:(h + 1) * gw, :] = jnp.concatenate([t_ref[h, a] for a in idx], axis=1)

    q = q_ref[...]
    head0 = lax.broadcasted_iota(jnp.int32, q.shape, 1) < dh
    zero = jnp.zeros_like(q)
    q0, q1 = jnp.where(head0, q, zero), jnp.where(head0, zero, q)
    qq = jnp.concatenate([t[dr * gw:(dr + 1) * gw] for dr in range(NA_QROWS) for t in (q0, q1)], axis=0)
    starts, entries = [], []
    for dr in range(NA_QROWS):
        r = blk * NA_QROWS + dr
        rs = jnp.clip(r - NA_KH // 2, 0, rows - NA_KH)
        starts.append(pl.multiple_of(rs * gw, gw))
        entries.append(jnp.where(blk == 0, dr, jnp.where(blk == nb - 1, NA_QROWS + 1 + dr, NA_QROWS)))
    pair = lambda dr: slice(2 * dr * gw, 2 * (dr + 1) * gw)
    s_ctx = _dot_nt(qq, kc_ref[...])
    s_loc = jnp.concatenate(
        [_dot_nt(qq[pair(dr)], k_ref[pl.ds(starts[dr], win), :]) + tab_ref[entries[dr]]
         for dr in range(NA_QROWS)], axis=0)
    m = jnp.maximum(jnp.max(s_loc, axis=-1, keepdims=True), jnp.max(s_ctx, axis=-1, keepdims=True))
    p_loc = jnp.exp(s_loc - m)
    p_ctx = jnp.exp(s_ctx - m)
    den = jnp.sum(p_loc, axis=-1, keepdims=True) + jnp.sum(p_ctx, axis=-1, keepdims=True)
    p_loc = p_loc.astype(_BF16)
    o = jnp.concatenate(
        [_dot(p_loc[pair(dr)], v_ref[pl.ds(starts[dr], win), :]) for dr in range(NA_QROWS)], axis=0)
    o = (o + _dot(p_ctx.astype(_BF16), vc_ref[...])) / den
    h0_lanes = lax.broadcasted_iota(jnp.int32, (gw, q.shape[1]), 1) < dh
    for dr in range(NA_QROWS):
        o_ref[dr * gw:(dr + 1) * gw, :] = jnp.where(
            h0_lanes, o[2 * dr * gw:(2 * dr + 1) * gw], o[(2 * dr + 1) * gw:(2 * dr + 2) * gw]).astype(_BF16)


def _na_latent(q, k, v, kc, vc, rpb):
    b, l, aw = q.shape
    lc = kc.shape[1]
    rows = l // GRID_W
    assert rows % NA_QROWS == 0 and rows >= 2 * NA_QROWS
    nb = rows // NA_QROWS
    hp = aw // (2 * NA_HEAD_DIM)
    tq = NA_QROWS * GRID_W
    row_entries = _na_row_entries(rows)
    tables = _na_column_tables(rpb.astype(_F32))
    tables = tables.reshape((hp, 2) + tables.shape[1:])
    return pl.pallas_call(
        functools.partial(_na_kernel, rows=rows, nb=nb, row_entries=row_entries),
        grid=(hp, b, nb),
        in_specs=[pl.BlockSpec((None, tq, 2 * NA_HEAD_DIM), lambda p, bi, j: (bi, j, p)),
                  pl.BlockSpec((None, l, 2 * NA_HEAD_DIM), lambda p, bi, j: (bi, 0, p)),
                  pl.BlockSpec((None, l, 2 * NA_HEAD_DIM), lambda p, bi, j: (bi, 0, p)),
                  pl.BlockSpec((None, lc, 2 * NA_HEAD_DIM), lambda p, bi, j: (bi, 0, p)),
                  pl.BlockSpec((None, lc, 2 * NA_HEAD_DIM), lambda p, bi, j: (bi, 0, p)),
                  pl.BlockSpec((None,) + tables.shape[1:], lambda p, bi, j: (p, 0, 0, 0, 0))],
        out_specs=pl.BlockSpec((None, tq, 2 * NA_HEAD_DIM), lambda p, bi, j: (bi, j, p)),
        out_shape=jax.ShapeDtypeStruct((b, l, aw), _BF16),
        scratch_shapes=[pltpu.VMEM((len(row_entries), 2 * GRID_W, NA_KH * GRID_W), _F32)],
        compiler_params=_cparams("arbitrary", "arbitrary", "arbitrary"),
        name="na_latent",
    )(q, k, v, kc, vc, tables)


def _ctx_attn_kernel(q_ref, k_ref, v_ref, o_ref):
    dh = NA_HEAD_DIM
    outs = []
    for h in range(2):
        ls = slice(h * dh, (h + 1) * dh)
        s = _dot_nt(q_ref[:, ls], k_ref[:, ls])
        m = jnp.max(s, axis=-1, keepdims=True)
        p = jnp.exp(s - m)
        den = jnp.sum(p, axis=-1, keepdims=True)
        outs.append(_dot(p.astype(_BF16), v_ref[:, ls]) / den)
    o_ref[...] = jnp.concatenate(outs, axis=-1).astype(_BF16)


def _ctx_attention(q, k, v):
    b, lc, aw = q.shape
    hp = aw // (2 * NA_HEAD_DIM)
    spec = pl.BlockSpec((None, lc, 2 * NA_HEAD_DIM), lambda bi, p: (bi, 0, p))
    return pl.pallas_call(
        _ctx_attn_kernel,
        grid=(b, hp),
        in_specs=[spec] * 3, out_specs=spec,
        out_shape=jax.ShapeDtypeStruct((b, lc, aw), _BF16),
        compiler_params=_cparams("parallel", "parallel"),
        name="ctx_attention",
    )(q, k, v)


def _gelu_tanh(x):
    return 0.5 * x * (1.0 + jnp.tanh(math.sqrt(2.0 / math.pi) * (x + 0.044715 * (x * x * x))))


def _merge_kernel(x_ref, mod_ref, g_ref, fa_ref, u_ref, yf_ref, yr_ref, nc_ref, d_ref, wglu_ref,
                  wga_ref, wgb_ref, wgc_ref, wa_ref, wb_ref, wc_ref, wo_ref, o_ref):
    x = x_ref[...]
    h = _norm_mod(x, g_ref[...], mod_ref[0:1, :], mod_ref[1:2, :]).astype(_BF16)
    y = u_ref[...].astype(_F32) * d_ref[...] + yf_ref[...].astype(_F32) + yr_ref[...].astype(_F32)
    gl = _gelu_tanh(y)
    sb = gl * jax.nn.sigmoid(_dot(gl.astype(_BF16), wglu_ref[...]))
    m = jax.nn.sigmoid(_dot(h, wga_ref[...])) * _dot(fa_ref[...], wa_ref[...])
    m += jax.nn.sigmoid(_dot(h, wgb_ref[...])) * _dot(sb.astype(_BF16), wb_ref[...])
    m += jax.nn.sigmoid(_dot(h, wgc_ref[...])) * _dot(nc_ref[...], wc_ref[...])
    o_ref[...] = x + mod_ref[2:3, :] * _dot(m.astype(_BF16), wo_ref[...])


def _merge(x, mod, g, fa, u, yf, yr, nc, d_skip, w_glu, w_in, w_a, w_b, w_c, w_out, layer, mod_row, off_g, tm):
    b, l, d = x.shape
    assert off_g % d == 0
    mod_spec, mat = _layer_specs(layer, mod_row, d)
    gate = lambda k: pl.BlockSpec((None, d, d), lambda bi, i: (layer, 0, off_g // d + k))
    tok = lambda a: pl.BlockSpec((None, tm, a.shape[2]), lambda bi, i: (bi, i, 0))
    tmaj = lambda a: pl.BlockSpec((tm, a.shape[1] // b), lambda bi, i: (i, bi))
    return pl.pallas_call(
        _merge_kernel,
        grid=(b, l // tm),
        in_specs=[tok(x), mod_spec, mat(g), tok(fa), tmaj(u), tmaj(yf), tmaj(yr), tok(nc), mat(d_skip), mat(w_glu),
                  gate(0), gate(1), gate(2), mat(w_a), mat(w_b), mat(w_c), mat(w_out)],
        out_specs=tok(x),
        out_shape=jax.ShapeDtypeStruct((b, l, d), _F32),
        compiler_params=_cparams("parallel", "parallel"),
        name="merge",
    )(x, mod, g, fa, u, yf, yr, nc, d_skip, w_glu, w_in, w_in, w_in, w_a, w_b, w_c, w_out)


def _ffn_kernel(x_ref, mod_ref, g_ref, w1_ref, w2_ref, gf_ref, o_ref, *, final_norm, fchunk):
    x = x_ref[...]
    h = _norm_mod(x, g_ref[...], mod_ref[3:4, :], mod_ref[4:5, :]).astype(_BF16)
    dff = w1_ref.shape[1]
    acc = jnp.zeros_like(x)
    for c in range(dff // fchunk):
        cs = slice(c * fchunk, (c + 1) * fchunk)
        t = jnp.maximum(_dot(h, w1_ref[:, cs]), 0.0)
        acc += _dot((t * t).astype(_BF16), w2_ref[cs, :])
    x2 = x + mod_ref[5:6, :] * acc
    if final_norm:
        ms = jnp.mean(x2 * x2, axis=-1, keepdims=True)
        x2 = x2 * lax.rsqrt(ms + RMS_EPS) * gf_ref[...]
    o_ref[...] = x2


def _ffn(x, mod, g, w1, w2, g_final, final_norm, layer, mod_row, tm):
    b, l, d = x.shape
    mod_spec, mat = _layer_specs(layer, mod_row, d)
    tok = pl.BlockSpec((None, tm, d), lambda bi, i: (bi, i, 0))
    full = lambda a: pl.BlockSpec(a.shape, lambda bi, i: (0, 0))
    return pl.pallas_call(
        functools.partial(_ffn_kernel, final_norm=final_norm, fchunk=1024),
        grid=(b, l // tm),
        in_specs=[tok, mod_spec, mat(g), mat(w1), mat(w2), full(g_final)],
        out_specs=tok,
        out_shape=jax.ShapeDtypeStruct((b, l, d), _F32),
        compiler_params=_cparams("parallel", "parallel"),
        name="ffn",
    )(x, mod, g, w1, w2, g_final)


def kernel(x, c, ctx, c_ctx, w_mod, b_mod, g_norm1, g_norm2, w_in, w_br_a, w_br_b, w_br_c, w_out, s5_lam_re, s5_lam_im, s5_log_dt, s5_b_re, s5_b_im, s5_c_re, s5_c_im, s5_d, s5_w_glu, na_rpb, w_ff1, w_ff2, g_final):
    b, l, d = x.shape
    lc = ctx.shape[1]
    depth = w_mod.shape[0]
    fw = w_br_a.shape[1]
    sw = w_br_b.shape[1]
    aw = w_br_c.shape[1]
    off_g = fw + sw + 3 * aw
    tm = 512
    tmc = lc

    c8 = jnp.concatenate([c, c_ctx[None, :], jnp.zeros((8 - b - 1, d), _F32)], axis=0)
    mod = _modulation(c8, w_mod, b_mod).reshape(depth, 8, 6, d)
    gf = g_final.reshape(1, d)
    g1 = g_norm1.reshape(depth, 1, d)
    g2 = g_norm2.reshape(depth, 1, d)
    d_skip = s5_d.reshape(depth, 1, sw)
    bf = lambda t: t.astype(_BF16)
    w_in_b, w_a, w_b, w_c, w_o, w_glu, w1, w2 = (bf(w_in), bf(w_br_a), bf(w_br_b), bf(w_br_c), bf(w_out),
                                                 bf(s5_w_glu), bf(w_ff1), bf(w_ff2))
    widths = (fw, sw, aw)

    for i in range(depth):
        need_ctx_out = i < depth - 1
        zf_x, zs_x, q_x, k_x, v_x = _inproj(x, mod, g1, w_in_b, i, None, widths, 2 * tm)
        zf_c, zs_c, q_c, k_c, v_c = _inproj(ctx, mod, g1, w_in_b, i, b, widths, tmc)

        fa_x = _fnet_latent(zf_x)
        (yf_x, yr_x), (yf_c, yr_c) = _s5_mix(zs_x, zs_c, b, s5_lam_re[i], s5_lam_im[i], s5_log_dt[i],
                                              s5_b_re[i], s5_b_im[i], s5_c_re[i], s5_c_im[i])
        nc_x = _na_latent(q_x, k_x, v_x, k_c, v_c, na_rpb[i])

        branch_w = (d_skip, w_glu, w_in_b, w_a, w_b, w_c, w_o)
        x = _merge(x, mod, g1, fa_x, zs_x, yf_x, yr_x, nc_x, *branch_w, i, None, off_g, tm)
        x = _ffn(x, mod, g2, w1, w2, gf, not need_ctx_out, i, None, tm)

        if need_ctx_out:
            fa_c = _fnet_ctx(zf_c)
            nc_c = _ctx_attention(q_c, k_c, v_c)
            ctx = _merge(ctx, mod, g1, fa_c, zs_c, yf_c, yr_c, nc_c, *branch_w, i, b, off_g, tmc)
            ctx = _ffn(ctx, mod, g2, w1, w2, gf, False, i, b, tmc)
    return x
```

```python
import functools
import math

import numpy as np
import jax
import jax.numpy as jnp
from jax import lax
from jax.experimental import pallas as pl
from jax.experimental.pallas import tpu as pltpu

_F32 = jnp.float32
_BF16 = jnp.bfloat16

RMS_EPS = 1e-6
GRID_W = 64
F_GROUP_DIM = 64
S5_GROUP_DIM = 16
NA_HEAD_DIM = 64
NA_KH = 8
NA_KW = 16
NA_QROWS = 8
NEG_BIG = -1e30
FFT_L2 = 128
S5_TS = 64

VMEM_LIMIT_BYTES = 56 * 1024 * 1024


def _cparams(*sem):
    return pltpu.CompilerParams(dimension_semantics=sem, vmem_limit_bytes=VMEM_LIMIT_BYTES)


def _dot(a, b):
    return jnp.dot(a, b, preferred_element_type=_F32)


def _table_bf16(t):
    return jnp.asarray(t, _F32).astype(_BF16)


def _dot_nt(a, b):
    return lax.dot_general(a, b, (((1,), (1,)), ((), ())), preferred_element_type=_F32)


def _norm_mod(x, g, shift, scale):
    ms = jnp.mean(x * x, axis=-1, keepdims=True)
    y = x * lax.rsqrt(ms + RMS_EPS) * g
    return y * (1.0 + scale) + shift


def _mod_kernel(c_ref, w_ref, b_ref, o_ref):
    c = c_ref[...]
    s = c * jax.nn.sigmoid(c)
    o_ref[...] = jnp.dot(s, w_ref[...], precision=lax.Precision.HIGHEST,
                         preferred_element_type=_F32) + b_ref[...]


def _modulation(c8, w_mod, b_mod):
    depth, d, n = w_mod.shape
    tn = 1024
    return pl.pallas_call(
        _mod_kernel,
        grid=(depth, n // tn),
        in_specs=[pl.BlockSpec((8, d), lambda l, j: (0, 0)),
                  pl.BlockSpec((None, d, tn), lambda l, j: (l, 0, j)),
                  pl.BlockSpec((None, 1, tn), lambda l, j: (l, 0, j))],
        out_specs=pl.BlockSpec((None, 8, tn), lambda l, j: (l, 0, j)),
        out_shape=jax.ShapeDtypeStruct((depth, 8, n), _F32),
        compiler_params=_cparams("parallel", "parallel"),
        name="modulation",
    )(c8, w_mod, b_mod.reshape(depth, 1, n))


def _inproj_kernel(x_ref, mod_ref, g_ref, w_ref, h_ref, zf_ref, zs_ref, q_ref, k_ref, v_ref, *, widths, q_scale):
    h = _norm_mod(x_ref[...], g_ref[...], mod_ref[0:1, :], mod_ref[1:2, :]).astype(_BF16)
    h_ref[...] = h
    fw, sw, aw = widths
    o = 0
    zf_ref[...] = _dot(h, w_ref[:, o:o + fw]).astype(_BF16)
    o += fw
    zs_ref[...] = _dot(h, w_ref[:, o:o + sw]).astype(_BF16)
    o += sw
    q_ref[...] = (_dot(h, w_ref[:, o:o + aw]) * q_scale).astype(_BF16)
    o += aw
    k_ref[...] = _dot(h, w_ref[:, o:o + aw]).astype(_BF16)
    o += aw
    v_ref[...] = _dot(h, w_ref[:, o:o + aw]).astype(_BF16)


def _layer_specs(layer, mod_row, d):
    mod = pl.BlockSpec((None, None, 6, d), lambda bi, i: (layer, bi if mod_row is None else mod_row, 0, 0))
    mat = lambda a: pl.BlockSpec((None,) + a.shape[1:], lambda bi, i: (layer, 0, 0))
    return mod, mat


def _inproj(x, mod, g, w, layer, mod_row, widths, tm):
    b, l, d = x.shape
    fw, sw, aw = widths
    n = fw + sw + 3 * aw
    mod_spec, mat = _layer_specs(layer, mod_row, d)
    tok = lambda width: pl.BlockSpec((None, tm, width), lambda bi, i: (bi, i, 0))
    return pl.pallas_call(
        functools.partial(_inproj_kernel, widths=widths, q_scale=NA_HEAD_DIM ** -0.5),
        grid=(b, l // tm),
        in_specs=[tok(d), mod_spec, mat(g), pl.BlockSpec((None, d, n), lambda bi, i: (layer, 0, 0))],
        out_specs=[tok(d), tok(fw), pl.BlockSpec((tm, sw), lambda bi, i: (i, bi)), tok(aw), tok(aw), tok(aw)],
        out_shape=[jax.ShapeDtypeStruct((b, l, d), _BF16), jax.ShapeDtypeStruct((b, l, fw), _BF16), jax.ShapeDtypeStruct((l, b * sw), _BF16),
                   jax.ShapeDtypeStruct((b, l, aw), _BF16), jax.ShapeDtypeStruct((b, l, aw), _BF16),
                   jax.ShapeDtypeStruct((b, l, aw), _BF16)],
        compiler_params=_cparams("parallel", "parallel"),
        name="inproj",
    )(x, mod, g, w)


def _dft_tables(l):
    l1, l2 = l // FFT_L2, FFT_L2
    a1 = 2 * np.pi * np.outer(np.arange(l1), np.arange(l1)) / l1
    f1 = np.concatenate([np.cos(a1), np.sin(a1)], axis=0)
    at = 2 * np.pi * np.outer(np.arange(l1), np.arange(l2)) / l
    a2 = 2 * np.pi * np.outer(np.arange(l2), np.arange(l2)) / l2
    c2, s2 = np.cos(a2), np.sin(a2)
    f2 = np.block([[c2, -s2], [s2, c2]])
    return f1, np.cos(at), np.sin(at), f2


def _group_dft_tables(width):
    ac = 2 * np.pi * np.outer(np.arange(F_GROUP_DIM), np.arange(F_GROUP_DIM)) / F_GROUP_DIM
    eye = np.eye(width // F_GROUP_DIM)
    return np.kron(eye, np.cos(ac)), np.kron(eye, np.sin(ac))


def _fnet_a_kernel(x_ref, f1_ref, cw_ref, sw_ref, o_ref):
    a = _dot(f1_ref[...], x_ref[...])
    l1 = a.shape[0] // 2
    ac, asn = a[:l1], a[l1:]
    cw, sw = cw_ref[...], sw_ref[...]
    o_ref[0] = (ac * cw - asn * sw).astype(_BF16)
    o_ref[1] = (ac * sw + asn * cw).astype(_BF16)


def _fnet_b_kernel(a_ref, f2_ref, bc_ref, bs_ref, o_ref, *, scale, nk1, width):
    half = f2_ref.shape[0] // 2
    for j in range(nk1):
        rhs = jnp.concatenate([a_ref[0, j], a_ref[1, j]], axis=0)
        r = _dot(f2_ref[...], rhs)
        out = _dot(r[:half].astype(_BF16), bc_ref[...]) - _dot(r[half:].astype(_BF16), bs_ref[...])
        o_ref[:, j * width:(j + 1) * width] = (out * scale).astype(_BF16)


def _fnet_latent(zf):
    b, l, width = zf.shape
    l1, l2 = l // FFT_L2, FFT_L2
    f1, cw, sw, f2 = _dft_tables(l)
    bc, bs = _group_dft_tables(width)
    nc = 16 * width
    cw = jnp.repeat(jnp.asarray(cw, _F32), width, axis=1)
    sw = jnp.repeat(jnp.asarray(sw, _F32), width, axis=1)
    a = pl.pallas_call(
        _fnet_a_kernel,
        grid=(l2 * width // nc, b),
        in_specs=[pl.BlockSpec((None, l1, nc), lambda j, bi: (bi, 0, j)),
                  pl.BlockSpec((2 * l1, l1), lambda j, bi: (0, 0)),
                  pl.BlockSpec((l1, nc), lambda j, bi: (0, j)),
                  pl.BlockSpec((l1, nc), lambda j, bi: (0, j))],
        out_specs=pl.BlockSpec((None, 2, l1, nc), lambda j, bi: (bi, 0, 0, j)),
        out_shape=jax.ShapeDtypeStruct((b, 2, l1, l2 * width), _BF16),
        compiler_params=_cparams("parallel", "parallel"),
        name="fnet_a",
    )(zf.reshape(b, l1, l2 * width), _table_bf16(f1), cw, sw)
    nk1 = 8
    out = pl.pallas_call(
        functools.partial(_fnet_b_kernel, scale=float((l * F_GROUP_DIM) ** -0.5), nk1=nk1, width=width),
        grid=(b, l1 // nk1),
        in_specs=[pl.BlockSpec((None, 2, nk1, l2, width), lambda bi, j: (bi, 0, j, 0, 0)),
                  pl.BlockSpec((2 * l2, 2 * l2), lambda bi, j: (0, 0)),
                  pl.BlockSpec((width, width), lambda bi, j: (0, 0)),
                  pl.BlockSpec((width, width), lambda bi, j: (0, 0))],
        out_specs=pl.BlockSpec((None, l2, nk1 * width), lambda bi, j: (bi, 0, j)),
        out_shape=jax.ShapeDtypeStruct((b, l2, l1 * width), _BF16),
        compiler_params=_cparams("parallel", "parallel"),
        name="fnet_b",
    )(a.reshape(b, 2, l1, l2, width), _table_bf16(f2), _table_bf16(bc), _table_bf16(bs))
    return out.reshape(b, l, width)


def _fnet_ctx_kernel(z_ref, cl_ref, sl_ref, bc_ref, bs_ref, o_ref, *, scale):
    z = z_ref[...]
    zc = _dot(z, bc_ref[...]).astype(_BF16)
    zs = _dot(z, bs_ref[...]).astype(_BF16)
    o_ref[...] = ((_dot(cl_ref[...], zc) - _dot(sl_ref[...], zs)) * scale).astype(_BF16)


def _fnet_ctx(zf):
    b, l, width = zf.shape
    al = 2 * np.pi * np.outer(np.arange(l), np.arange(l)) / l
    bc, bs = _group_dft_tables(width)
    full = lambda r, c: pl.BlockSpec((r, c), lambda bi: (0, 0))
    return pl.pallas_call(
        functools.partial(_fnet_ctx_kernel, scale=float((l * F_GROUP_DIM) ** -0.5)),
        grid=(b,),
        in_specs=[pl.BlockSpec((None, l, width), lambda bi: (bi, 0, 0)),
                  full(l, l), full(l, l), full(width, width), full(width, width)],
        out_specs=pl.BlockSpec((None, l, width), lambda bi: (bi, 0, 0)),
        out_shape=jax.ShapeDtypeStruct((b, l, width), _BF16),
        compiler_params=_cparams("parallel"),
        name="fnet_ctx",
    )(zf, _table_bf16(np.cos(al)), _table_bf16(np.sin(al)), _table_bf16(bc), _table_bf16(bs))


def _s5_disc_kernel(lr_ref, li_ref, ldt_ref, br_ref, bi_ref, lbr_ref, lbi_ref, bbr_ref, bbi_ref):
    lr = jnp.minimum(lr_ref[...], -1e-4)
    li = li_ref[...]
    dt = jnp.exp(ldt_ref[...])
    mag = jnp.exp(lr * dt)
    lbr = mag * jnp.cos(li * dt)
    lbi = mag * jnp.sin(li * dt)
    lbr_ref[...] = lbr
    lbi_ref[...] = lbi
    a, bb = lbr - 1.0, lbi
    den = lr * lr + li * li
    cr = (a * lr + bb * li) / den
    ci = (bb * lr - a * li) / den
    br, bi = br_ref[...], bi_ref[...]
    bbr_ref[...] = cr * br - ci * bi
    bbi_ref[...] = cr * bi + ci * br


def _s5_discretise(lam_re, lam_im, log_dt, b_re, b_im):
    _, g, p, n = b_re.shape
    rows = 2 * g * n
    ex = lambda t: jnp.broadcast_to(t[:, :, None, :], (2, g, n, p)).reshape(rows, p)
    ldt = jnp.broadcast_to(log_dt[:, :, None, None], (2, g, n, p)).reshape(rows, p)
    tb = lambda t: t.transpose(0, 1, 3, 2).reshape(rows, p)
    spec = pl.BlockSpec((rows, p), lambda: (0, 0))
    sds = jax.ShapeDtypeStruct((rows, p), _F32)
    lbr, lbi, bbr, bbi = pl.pallas_call(
        _s5_disc_kernel,
        in_specs=[spec] * 5, out_specs=[spec] * 4, out_shape=[sds] * 4,
        name="s5_discretise",
    )(ex(lam_re), ex(lam_im), ldt, tb(b_re), tb(b_im))
    shp = (2, g, n, p)
    return lbr.reshape(shp)[:, :, 0, :], lbi.reshape(shp)[:, :, 0, :], bbr.reshape(shp), bbi.reshape(shp)


def _s5_scan_kernel(ucf_ref, uxf_ref, ucr_ref, uxr_ref, perm_ref, permt_ref, bf_ref, br_ref, l_re_ref, l_im_ref,
                    cf_re_ref, cf_im_ref, cr_re_ref, cr_im_ref,
                    yfc_ref, yfx_ref, yrc_ref, yrx_ref,
                    d0a, d0b, d0c, d0d, d1a, d1b, d1c, d1d, h0a, h0b, h0c, h0d, h1a, h1b, h1c, h1d,
                    st_re, st_im, *, ts, nb, nc2, nst, sw):
    g = pl.program_id(0)
    d0, d1 = (d0a, d0b, d0c, d0d), (d1a, d1b, d1c, d1d)
    h0, h1 = (h0a, h0b, h0c, h0d), (h1a, h1b, h1c, h1d)

    @pl.when(g == 0)
    def _():
        for r in d0 + d1 + h0 + h1 + (st_re, st_im):
            r[...] = jnp.zeros_like(r)

    ntile = ts // 2
    lower = lax.broadcasted_iota(jnp.int32, (8, nst), 0) < nb
    is_ctx_in = g < nc2
    u_f = jnp.where(is_ctx_in, ucf_ref[...], uxf_ref[...])
    u_r = jnp.where(is_ctx_in, ucr_ref[...], uxr_ref[...])

    def time_batch_rows(u):
        stack = jnp.concatenate([u[:, b * sw:(b + 1) * sw] for b in range(nb)], axis=0)
        return _dot(perm_ref[...], stack).astype(_BF16)

    def drive(uf, ur, d):
        uf, ur = time_batch_rows(uf), time_batch_rows(ur)
        d[0][...] = _dot(uf, bf_ref[:, 0:nst])
        d[1][...] = _dot(uf, bf_ref[:, nst:2 * nst])
        d[2][...] = _dot(ur, br_ref[:, 0:nst])
        d[3][...] = _dot(ur, br_ref[:, nst:2 * nst])

    def scan(d, h):
        l_re, l_im = l_re_ref[...], l_im_ref[...]
        x_re, x_im = st_re[...], st_im[...]

        def step(x_re, x_im, s_re, s_im):
            return l_re * x_re - l_im * x_im + s_re, l_re * x_im + l_im * x_re + s_im

        for k in range(ntile // 2):
            f_tiles, r_tiles = [], []
            for m in (2 * k, 2 * k + 1):
                tf = slice(8 * m, 8 * m + 8)
                tr = slice(8 * (ntile - 1 - m), 8 * (ntile - m))
                a_re, a_im, b_re, b_im = d[0][tf, :], d[1][tf, :], d[2][tr, :], d[3][tr, :]
                e_re, e_im = step(x_re, x_im, jnp.where(lower, a_re, b_re), jnp.where(lower, a_im, b_im))
                x_re, x_im = step(e_re, e_im, pltpu.roll(jnp.where(lower, b_re, a_re), nb, 0),
                                  pltpu.roll(jnp.where(lower, b_im, a_im), nb, 0))
                o_re, o_im = pltpu.roll(x_re, nb, 0), pltpu.roll(x_im, nb, 0)
                f_tiles.append((jnp.where(lower, e_re, o_re), jnp.where(lower, e_im, o_im)))
                r_tiles.append((jnp.where(lower, o_re, e_re), jnp.where(lower, o_im, e_im)))
            pf = slice(16 * k, 16 * k + 16)
            pr = slice(16 * (ntile // 2 - 1 - k), 16 * (ntile // 2 - k))
            pack = lambda lo_t, hi_t: jnp.concatenate([lo_t, hi_t], axis=0).astype(_BF16)
            h[0][pf, :] = pack(f_tiles[0][0], f_tiles[1][0])
            h[1][pf, :] = pack(f_tiles[0][1], f_tiles[1][1])
            h[2][pr, :] = pack(r_tiles[1][0], r_tiles[0][0])
            h[3][pr, :] = pack(r_tiles[1][1], r_tiles[0][1])
        st_re[...] = x_re
        st_im[...] = x_im

    def readout(h):
        yf = (_dot(h[0][...], cf_re_ref[...]) - _dot(h[1][...], cf_im_ref[...])).astype(_BF16)
        yr = (_dot(h[2][...], cr_re_ref[...]) - _dot(h[3][...], cr_im_ref[...])).astype(_BF16)
        return _dot(permt_ref[...], yf).astype(_BF16), _dot(permt_ref[...], yr).astype(_BF16)

    lo, hi = slice(0, ts), slice(ts, 2 * ts)
    scan(d1, h1)
    drive(u_f[lo], u_r[hi], d0)
    yf0, yr0 = readout(h0)
    drive(u_f[hi], u_r[lo], d1)
    scan(d0, h0)
    yf1, yr1 = readout(h1)

    def store(yf_ref, yr_ref):
        for b in range(nb):
            cs = slice(b * sw, (b + 1) * sw)
            rows = slice(b * ts, (b + 1) * ts)
            yf_ref[lo, cs] = yf0[rows]
            yf_ref[hi, cs] = yf1[rows]
            yr_ref[hi, cs] = yr0[rows]
            yr_ref[lo, cs] = yr1[rows]

    pl.when(jnp.logical_and(g >= 1, g <= nc2))(lambda: store(yfc_ref, yrc_ref))
    pl.when(g > nc2)(lambda: store(yfx_ref, yrx_ref))


def _s5_scan(u_c, u_x, bf, br, l_re, l_im, cf_re, cf_im, cr_re, cr_im, nb, ts):
    sw = u_x.shape[1] // nb
    nst = l_re.shape[1]
    blk = 2 * ts
    nc2, nx2 = u_c.shape[0] // blk, u_x.shape[0] // blk
    nt2 = nc2 + nx2
    perm = np.zeros((ts * nb, ts * nb), np.float32)
    for t in range(ts):
        for b in range(nb):
            perm[t * nb + b, b * ts + t] = 1.0
    full = lambda a: pl.BlockSpec(a.shape, lambda g: (0, 0))
    spec = lambda f: pl.BlockSpec((blk, nb * sw), lambda g: (f(g), 0))
    clip = lambda v, n: jnp.clip(v, 0, n - 1)
    ins = [spec(lambda g: clip(g, nc2)), spec(lambda g: clip(g - nc2, nx2)),
           spec(lambda g: clip(nc2 - 1 - g, nc2)), spec(lambda g: clip(nt2 - 1 - g, nx2))]
    outs = [spec(lambda g: clip(g - 1, nc2)), spec(lambda g: clip(g - 1 - nc2, nx2)),
            spec(lambda g: clip(nc2 - g, nc2)), spec(lambda g: clip(nt2 - g, nx2))]
    perm_b, permt_b = _table_bf16(perm), _table_bf16(perm.T)
    sds = lambda a: jax.ShapeDtypeStruct(a.shape, _BF16)
    return pl.pallas_call(
        functools.partial(_s5_scan_kernel, ts=ts, nb=nb, nc2=nc2, nst=nst, sw=sw),
        grid=(nt2 + 1,),
        in_specs=ins + [full(perm_b), full(permt_b), full(bf), full(br), full(l_re), full(l_im),
                        full(cf_re), full(cf_im), full(cr_re), full(cr_im)],
        out_specs=outs,
        out_shape=[sds(u_c), sds(u_x), sds(u_c), sds(u_x)],
        scratch_shapes=([pltpu.VMEM((ts * nb, nst), _F32)] * 8 + [pltpu.VMEM((ts * nb, nst), _BF16)] * 8
                        + [pltpu.VMEM((8, nst), _F32)] * 2),
        compiler_params=_cparams("arbitrary"),
        name="s5_scan",
    )(u_c, u_x, u_c, u_x, perm_b, permt_b, bf, br, l_re, l_im, cf_re, cf_im, cr_re, cr_im)


def _block_diag(t):
    g, r, c = t.shape
    eye = jnp.eye(g, dtype=t.dtype)
    return (t[:, :, None, :] * eye[:, None, :, None]).reshape(g * r, g * c)


def _s5_mix(zs_x, zs_c, nb, lam_re, lam_im, log_dt, b_re, b_im, c_re, c_im):
    assert 2 * nb == 8, "the scan packs batch x direction on the 8 sublanes"
    lbr, lbi, bbr, bbi = _s5_discretise(lam_re, lam_im, log_dt, b_re, b_im)
    nst = lbr.shape[1] * lbr.shape[2]
    drive = lambda d: jnp.concatenate([_block_diag(bbr[d]), _block_diag(bbi[d])], axis=1).astype(_BF16)
    rd = lambda t: _block_diag(t.transpose(0, 2, 1)).astype(_BF16)
    tile = lambda t: jnp.concatenate([jnp.broadcast_to(t[0].reshape(1, nst), (nb, nst)),
                                      jnp.broadcast_to(t[1].reshape(1, nst), (nb, nst))], axis=0)
    yf_c, yf_x, yr_c, yr_x = _s5_scan(zs_c, zs_x, drive(0), drive(1), tile(lbr), tile(lbi),
                                      rd(c_re[0]), rd(c_im[0]), rd(c_re[1]), rd(c_im[1]), nb, S5_TS)
    return (yf_x, yr_x), (yf_c, yr_c)


def _na_row_entries(rows):
    r = np.concatenate([np.arange(NA_QROWS), [NA_QROWS], rows - NA_QROWS + np.arange(NA_QROWS)])
    rs = np.clip(r - NA_KH // 2, 0, rows - NA_KH)
    idx = rs[:, None] + np.arange(NA_KH)[None, :] - r[:, None] + NA_KH - 1
    return tuple(tuple(int(v) for v in row) for row in idx)


def _na_table_kernel(rpb_ref, sel_ref, valid_ref, o_ref):
    t = jnp.dot(rpb_ref[...], sel_ref[...], precision=lax.Precision.HIGHEST, preferred_element_type=_F32)
    o_ref[...] = jnp.where(valid_ref[...] > 0.0, t, NEG_BIG)


def _na_column_tables(rpb):
    h, nr, ncol = rpb.shape
    w = np.arange(GRID_W)
    kc = np.arange(GRID_W)
    cs = np.clip(w - NA_KW // 2, 0, GRID_W - NA_KW)
    valid_c = (kc[None, :] >= cs[:, None]) & (kc[None, :] < cs[:, None] + NA_KW)
    idx_c = np.clip(kc[None, :] - w[:, None] + NA_KW - 1, 0, ncol - 1).reshape(-1)
    kpad = -(-ncol // 8) * 8
    sel = np.zeros((kpad, GRID_W * GRID_W), np.float32)
    sel[idx_c, np.arange(GRID_W * GRID_W)] = 1.0
    full = lambda r, c: pl.BlockSpec((r, c), lambda: (0, 0))
    out = pl.pallas_call(
        _na_table_kernel,
        in_specs=[full(h * nr, kpad), full(kpad, GRID_W * GRID_W), full(1, GRID_W * GRID_W)],
        out_specs=full(h * nr, GRID_W * GRID_W),
        out_shape=jax.ShapeDtypeStruct((h * nr, GRID_W * GRID_W), _F32),
        name="na_tables",
    )(jnp.pad(rpb.reshape(h * nr, ncol), ((0, 0), (0, kpad - ncol))), jnp.asarray(sel),
      jnp.asarray(valid_c.reshape(1, -1).astype(np.float32)))
    return out.reshape(h, nr, GRID_W, GRID_W)


def _na_kernel(q_ref, k_ref, v_ref, kc_ref, vc_ref, t_ref, o_ref, tab_ref, *, rows, nb, row_entries):
    blk = pl.program_id(2)
    dh = NA_HEAD_DIM
    win = NA_KH * GRID_W
    gw = GRID_W

    @pl.when(jnp.logical_and(pl.program_id(1) == 0, blk == 0))
    def _():
        for e, idx in enumerate(row_entries):
            for h in range(2):
                tab_ref[e, h * gw:(h + 1) * gw, :] = jnp.concatenate([t_ref[h, a] for a in idx], axis=1)

    q = q_ref[...]
    head0 = lax.broadcasted_iota(jnp.int32, q.shape, 1) < dh
    zero = jnp.zeros_like(q)
    q0, q1 = jnp.where(head0, q, zero), jnp.where(head0, zero, q)
    qq = jnp.concatenate([t[dr * gw:(dr + 1) * gw] for dr in range(NA_QROWS) for t in (q0, q1)], axis=0)
    starts, entries = [], []
    for dr in range(NA_QROWS):
        r = blk * NA_QROWS + dr
        rs = jnp.clip(r - NA_KH // 2, 0, rows - NA_KH)
        starts.append(pl.multiple_of(rs * gw, gw))
        entries.append(jnp.where(blk == 0, dr, jnp.where(blk == nb - 1, NA_QROWS + 1 + dr, NA_QROWS)))
    pair = lambda dr: slice(2 * dr * gw, 2 * (dr + 1) * gw)
    s_ctx = _dot_nt(qq, kc_ref[...])
    s_loc = jnp.concatenate(
        [_dot_nt(qq[pair(dr)], k_ref[pl.ds(starts[dr], win), :]) + tab_ref[entries[dr]]
         for dr in range(NA_QROWS)], axis=0)
    m = jnp.maximum(jnp.max(s_loc, axis=-1, keepdims=True), jnp.max(s_ctx, axis=-1, keepdims=True))
    p_loc = jnp.exp(s_loc - m)
    p_ctx = jnp.exp(s_ctx - m)
    den = jnp.sum(p_loc, axis=-1, keepdims=True) + jnp.sum(p_ctx, axis=-1, keepdims=True)
    p_loc = p_loc.astype(_BF16)
    o = jnp.concatenate(
        [_dot(p_loc[pair(dr)], v_ref[pl.ds(starts[dr], win), :]) for dr in range(NA_QROWS)], axis=0)
    o = (o + _dot(p_ctx.astype(_BF16), vc_ref[...])) / den
    h0_lanes = lax.broadcasted_iota(jnp.int32, (gw, q.shape[1]), 1) < dh
    for dr in range(NA_QROWS):
        o_ref[dr * gw:(dr + 1) * gw, :] = jnp.where(
            h0_lanes, o[2 * dr * gw:(2 * dr + 1) * gw], o[(2 * dr + 1) * gw:(2 * dr + 2) * gw]).astype(_BF16)


def _na_latent(q, k, v, kc, vc, rpb):
    b, l, aw = q.shape
    lc = kc.shape[1]
    rows = l // GRID_W
    assert rows % NA_QROWS == 0 and rows >= 2 * NA_QROWS
    nb = rows // NA_QROWS
    hp = aw // (2 * NA_HEAD_DIM)
    tq = NA_QROWS * GRID_W
    row_entries = _na_row_entries(rows)
    tables = _na_column_tables(rpb.astype(_F32))
    tables = tables.reshape((hp, 2) + tables.shape[1:])
    return pl.pallas_call(
        functools.partial(_na_kernel, rows=rows, nb=nb, row_entries=row_entries),
        grid=(hp, b, nb),
        in_specs=[pl.BlockSpec((None, tq, 2 * NA_HEAD_DIM), lambda p, bi, j: (bi, j, p)),
                  pl.BlockSpec((None, l, 2 * NA_HEAD_DIM), lambda p, bi, j: (bi, 0, p)),
                  pl.BlockSpec((None, l, 2 * NA_HEAD_DIM), lambda p, bi, j: (bi, 0, p)),
                  pl.BlockSpec((None, lc, 2 * NA_HEAD_DIM), lambda p, bi, j: (bi, 0, p)),
                  pl.BlockSpec((None, lc, 2 * NA_HEAD_DIM), lambda p, bi, j: (bi, 0, p)),
                  pl.BlockSpec((None,) + tables.shape[1:], lambda p, bi, j: (p, 0, 0, 0, 0))],
        out_specs=pl.BlockSpec((None, tq, 2 * NA_HEAD_DIM), lambda p, bi, j: (bi, j, p)),
        out_shape=jax.ShapeDtypeStruct((b, l, aw), _BF16),
        scratch_shapes=[pltpu.VMEM((len(row_entries), 2 * GRID_W, NA_KH * GRID_W), _F32)],
        compiler_params=_cparams("arbitrary", "arbitrary", "arbitrary"),
        name="na_latent",
    )(q, k, v, kc, vc, tables)


def _ctx_attn_kernel(q_ref, k_ref, v_ref, o_ref):
    dh = NA_HEAD_DIM
    outs = []
    for h in range(2):
        ls = slice(h * dh, (h + 1) * dh)
        s = _dot_nt(q_ref[:, ls], k_ref[:, ls])
        m = jnp.max(s, axis=-1, keepdims=True)
        p = jnp.exp(s - m)
        den = jnp.sum(p, axis=-1, keepdims=True)
        outs.append(_dot(p.astype(_BF16), v_ref[:, ls]) / den)
    o_ref[...] = jnp.concatenate(outs, axis=-1).astype(_BF16)


def _ctx_attention(q, k, v):
    b, lc, aw = q.shape
    hp = aw // (2 * NA_HEAD_DIM)
    spec = pl.BlockSpec((None, lc, 2 * NA_HEAD_DIM), lambda bi, p: (bi, 0, p))
    return pl.pallas_call(
        _ctx_attn_kernel,
        grid=(b, hp),
        in_specs=[spec] * 3, out_specs=spec,
        out_shape=jax.ShapeDtypeStruct((b, lc, aw), _BF16),
        compiler_params=_cparams("parallel", "parallel"),
        name="ctx_attention",
    )(q, k, v)


def _sigmoid(z):
    return 0.5 * jnp.tanh(0.5 * z) + 0.5


def _gelu_tanh(x):
    return 0.5 * x * (1.0 + jnp.tanh(math.sqrt(2.0 / math.pi) * (x + 0.044715 * (x * x * x))))


def _merge_kernel(x_ref, mod_ref, h_ref, fa_ref, u_ref, yf_ref, yr_ref, nc_ref, d_ref, wglu_ref,
                  wga_ref, wgb_ref, wgc_ref, wa_ref, wb_ref, wc_ref, wo_ref, o_ref):
    x = x_ref[...]
    h = h_ref[...]
    y = u_ref[...].astype(_F32) * d_ref[...] + yf_ref[...].astype(_F32) + yr_ref[...].astype(_F32)
    gl = _gelu_tanh(y)
    sb = gl * _sigmoid(_dot(gl.astype(_BF16), wglu_ref[...]))
    m = _sigmoid(_dot(h, wga_ref[...])) * _dot(fa_ref[...], wa_ref[...])
    m += _sigmoid(_dot(h, wgb_ref[...])) * _dot(sb.astype(_BF16), wb_ref[...])
    m += _sigmoid(_dot(h, wgc_ref[...])) * _dot(nc_ref[...], wc_ref[...])
    o_ref[...] = x + mod_ref[2:3, :] * _dot(m.astype(_BF16), wo_ref[...])


def _merge(x, mod, h, fa, u, yf, yr, nc, d_skip, w_glu, w_in, w_a, w_b, w_c, w_out, layer, mod_row, off_g, tm):
    b, l, d = x.shape
    assert off_g % d == 0
    mod_spec, mat = _layer_specs(layer, mod_row, d)
    gate = lambda k: pl.BlockSpec((None, d, d), lambda bi, i: (layer, 0, off_g // d + k))
    tok = lambda a: pl.BlockSpec((None, tm, a.shape[2]), lambda bi, i: (bi, i, 0))
    tmaj = lambda a: pl.BlockSpec((tm, a.shape[1] // b), lambda bi, i: (i, bi))
    return pl.pallas_call(
        _merge_kernel,
        grid=(b, l // tm),
        in_specs=[tok(x), mod_spec, tok(h), tok(fa), tmaj(u), tmaj(yf), tmaj(yr), tok(nc), mat(d_skip), mat(w_glu),
                  gate(0), gate(1), gate(2), mat(w_a), mat(w_b), mat(w_c), mat(w_out)],
        out_specs=tok(x),
        out_shape=jax.ShapeDtypeStruct((b, l, d), _F32),
        compiler_params=_cparams("parallel", "parallel"),
        name="merge",
    )(x, mod, h, fa, u, yf, yr, nc, d_skip, w_glu, w_in, w_in, w_in, w_a, w_b, w_c, w_out)


def _ffn_kernel(x_ref, mod_ref, g_ref, w1_ref, w2_ref, gf_ref, o_ref, *, final_norm, fchunk):
    x = x_ref[...]
    h = _norm_mod(x, g_ref[...], mod_ref[3:4, :], mod_ref[4:5, :]).astype(_BF16)
    dff = w1_ref.shape[1]
    acc = jnp.zeros_like(x)
    for c in range(dff // fchunk):
        cs = slice(c * fchunk, (c + 1) * fchunk)
        t = jnp.maximum(_dot(h, w1_ref[:, cs]), 0.0)
        acc += _dot((t * t).astype(_BF16), w2_ref[cs, :])
    x2 = x + mod_ref[5:6, :] * acc
    if final_norm:
        ms = jnp.mean(x2 * x2, axis=-1, keepdims=True)
        x2 = x2 * lax.rsqrt(ms + RMS_EPS) * gf_ref[...]
    o_ref[...] = x2


def _ffn(x, mod, g, w1, w2, g_final, final_norm, layer, mod_row, tm):
    b, l, d = x.shape
    mod_spec, mat = _layer_specs(layer, mod_row, d)
    tok = pl.BlockSpec((None, tm, d), lambda bi, i: (bi, i, 0))
    full = lambda a: pl.BlockSpec(a.shape, lambda bi, i: (0, 0))
    return pl.pallas_call(
        functools.partial(_ffn_kernel, final_norm=final_norm, fchunk=1024),
        grid=(b, l // tm),
        in_specs=[tok, mod_spec, mat(g), mat(w1), mat(w2), full(g_final)],
        out_specs=tok,
        out_shape=jax.ShapeDtypeStruct((b, l, d), _F32),
        compiler_params=_cparams("parallel", "parallel"),
        name="ffn",
    )(x, mod, g, w1, w2, g_final)


def kernel(x, c, ctx, c_ctx, w_mod, b_mod, g_norm1, g_norm2, w_in, w_br_a, w_br_b, w_br_c, w_out, s5_lam_re, s5_lam_im, s5_log_dt, s5_b_re, s5_b_im, s5_c_re, s5_c_im, s5_d, s5_w_glu, na_rpb, w_ff1, w_ff2, g_final):
    b, l, d = x.shape
    lc = ctx.shape[1]
    depth = w_mod.shape[0]
    fw = w_br_a.shape[1]
    sw = w_br_b.shape[1]
    aw = w_br_c.shape[1]
    off_g = fw + sw + 3 * aw
    tm = 512
    tmc = lc

    c8 = jnp.concatenate([c, c_ctx[None, :], jnp.zeros((8 - b - 1, d), _F32)], axis=0)
    mod = _modulation(c8, w_mod, b_mod).reshape(depth, 8, 6, d)
    gf = g_final.reshape(1, d)
    g1 = g_norm1.reshape(depth, 1, d)
    g2 = g_norm2.reshape(depth, 1, d)
    d_skip = s5_d.reshape(depth, 1, sw)
    bf = lambda t: t.astype(_BF16)
    w_in_b, w_a, w_b, w_c, w_o, w_glu, w1, w2 = (bf(w_in), bf(w_br_a), bf(w_br_b), bf(w_br_c), bf(w_out),
                                                 bf(s5_w_glu), bf(w_ff1), bf(w_ff2))
    widths = (fw, sw, aw)

    for i in range(depth):
        need_ctx_out = i < depth - 1
        h_x, zf_x, zs_x, q_x, k_x, v_x = _inproj(x, mod, g1, w_in_b, i, None, widths, 2 * tm)
        h_c, zf_c, zs_c, q_c, k_c, v_c = _inproj(ctx, mod, g1, w_in_b, i, b, widths, tmc)

        fa_x = _fnet_latent(zf_x)
        (yf_x, yr_x), (yf_c, yr_c) = _s5_mix(zs_x, zs_c, b, s5_lam_re[i], s5_lam_im[i], s5_log_dt[i],
                                              s5_b_re[i], s5_b_im[i], s5_c_re[i], s5_c_im[i])
        nc_x = _na_latent(q_x, k_x, v_x, k_c, v_c, na_rpb[i])

        branch_w = (d_skip, w_glu, w_in_b, w_a, w_b, w_c, w_o)
        x = _merge(x, mod, h_x, fa_x, zs_x, yf_x, yr_x, nc_x, *branch_w, i, None, off_g, 2 * tm)
        x = _ffn(x, mod, g2, w1, w2, gf, not need_ctx_out, i, None, tm)

        if need_ctx_out:
            fa_c = _fnet_ctx(zf_c)
            nc_c = _ctx_attention(q_c, k_c, v_c)
            ctx = _merge(ctx, mod, h_c, fa_c, zs_c, yf_c, yr_c, nc_c, *branch_w, i, b, off_g, tmc)
            ctx = _ffn(ctx, mod, g2, w1, w2, gf, False, i, b, tmc)
    return x
```

```python
import functools
import math

import numpy as np
import jax
import jax.numpy as jnp
from jax import lax
from jax.experimental import pallas as pl
from jax.experimental.pallas import tpu as pltpu

_F32 = jnp.float32
_BF16 = jnp.bfloat16

RMS_EPS = 1e-6
GRID_W = 64
F_GROUP_DIM = 64
NA_HEAD_DIM = 64
NA_KH = 8
NA_KW = 16
NA_QROWS = 8
NEG_BIG = -1e30
FFT_L2 = 128
FFT_T2_STEP = 16
FFT_K1_STEP = 8
S5_TS = 64
TOKEN_TILE = 1024
FFN_CHUNK = 1024
MOD_TN = 1024

SUBLANES = 8
VMEM_LIMIT_BYTES = 56 * 1024 * 1024


def _cparams(*sem):
    return pltpu.CompilerParams(dimension_semantics=sem, vmem_limit_bytes=VMEM_LIMIT_BYTES)


def _dot(a, b):
    return jnp.dot(a, b, preferred_element_type=_F32)


def _table_bf16(t):
    return jnp.asarray(t, _F32).astype(_BF16)


def _dot_nt(a, b):
    return lax.dot_general(a, b, (((1,), (1,)), ((), ())), preferred_element_type=_F32)


def _norm_mod(x, g, shift, scale):
    ms = jnp.mean(x * x, axis=-1, keepdims=True)
    y = x * lax.rsqrt(ms + RMS_EPS) * g
    return y * (1.0 + scale) + shift


def _mod_kernel(c_ref, w_ref, b_ref, o_ref):
    c = c_ref[...]
    s = c * jax.nn.sigmoid(c)
    o_ref[...] = jnp.dot(s, w_ref[...], precision=lax.Precision.HIGHEST,
                         preferred_element_type=_F32) + b_ref[...]


def _modulation(c8, w_mod, b_mod):
    depth, d, n = w_mod.shape
    tn = MOD_TN
    rows = c8.shape[0]
    return pl.pallas_call(
        _mod_kernel,
        grid=(depth, n // tn),
        in_specs=[pl.BlockSpec((rows, d), lambda l, j: (0, 0)),
                  pl.BlockSpec((None, d, tn), lambda l, j: (l, 0, j)),
                  pl.BlockSpec((None, 1, tn), lambda l, j: (l, 0, j))],
        out_specs=pl.BlockSpec((None, rows, tn), lambda l, j: (l, 0, j)),
        out_shape=jax.ShapeDtypeStruct((depth, rows, n), _F32),
        compiler_params=_cparams("parallel", "parallel"),
        name="modulation",
    )(c8, w_mod, b_mod.reshape(depth, 1, n))


def _inproj_kernel(x_ref, mod_ref, g_ref, w_ref, h_ref, zf_ref, zs_ref, q_ref, k_ref, v_ref, *, widths, q_scale):
    h = _norm_mod(x_ref[...], g_ref[...], mod_ref[0:1, :], mod_ref[1:2, :]).astype(_BF16)
    h_ref[...] = h
    fw, sw, aw = widths
    o = 0
    zf_ref[...] = _dot(h, w_ref[:, o:o + fw]).astype(_BF16)
    o += fw
    zs_ref[...] = _dot(h, w_ref[:, o:o + sw]).astype(_BF16)
    o += sw
    q_ref[...] = (_dot(h, w_ref[:, o:o + aw]) * q_scale).astype(_BF16)
    o += aw
    k_ref[...] = _dot(h, w_ref[:, o:o + aw]).astype(_BF16)
    o += aw
    v_ref[...] = _dot(h, w_ref[:, o:o + aw]).astype(_BF16)


def _layer_specs(layer, mod_row, d):
    mod = pl.BlockSpec((None, None, 6, d), lambda bi, i: (layer, bi if mod_row is None else mod_row, 0, 0))
    mat = lambda a: pl.BlockSpec((None,) + a.shape[1:], lambda bi, i: (layer, 0, 0))
    return mod, mat


def _inproj(x, mod, g, w, layer, mod_row, widths, tm):
    b, l, d = x.shape
    fw, sw, aw = widths
    n = fw + sw + 3 * aw
    mod_spec, mat = _layer_specs(layer, mod_row, d)
    tok = lambda width: pl.BlockSpec((None, tm, width), lambda bi, i: (bi, i, 0))
    return pl.pallas_call(
        functools.partial(_inproj_kernel, widths=widths, q_scale=NA_HEAD_DIM ** -0.5),
        grid=(b, l // tm),
        in_specs=[tok(d), mod_spec, mat(g), pl.BlockSpec((None, d, n), lambda bi, i: (layer, 0, 0))],
        out_specs=[tok(d), tok(fw), pl.BlockSpec((tm, sw), lambda bi, i: (i, bi)), tok(aw), tok(aw), tok(aw)],
        out_shape=[jax.ShapeDtypeStruct((b, l, d), _BF16), jax.ShapeDtypeStruct((b, l, fw), _BF16), jax.ShapeDtypeStruct((l, b * sw), _BF16),
                   jax.ShapeDtypeStruct((b, l, aw), _BF16), jax.ShapeDtypeStruct((b, l, aw), _BF16),
                   jax.ShapeDtypeStruct((b, l, aw), _BF16)],
        compiler_params=_cparams("parallel", "parallel"),
        name="inproj",
    )(x, mod, g, w)


def _dft_tables(l):
    l1, l2 = l // FFT_L2, FFT_L2
    a1 = 2 * np.pi * np.outer(np.arange(l1), np.arange(l1)) / l1
    f1 = np.concatenate([np.cos(a1), np.sin(a1)], axis=0)
    at = 2 * np.pi * np.outer(np.arange(l1), np.arange(l2)) / l
    a2 = 2 * np.pi * np.outer(np.arange(l2), np.arange(l2)) / l2
    c2, s2 = np.cos(a2), np.sin(a2)
    f2 = np.block([[c2, -s2], [s2, c2]])
    return f1, np.cos(at), np.sin(at), f2


def _group_dft_tables(width):
    ac = 2 * np.pi * np.outer(np.arange(F_GROUP_DIM), np.arange(F_GROUP_DIM)) / F_GROUP_DIM
    eye = np.eye(width // F_GROUP_DIM)
    return np.kron(eye, np.cos(ac)), np.kron(eye, np.sin(ac))


def _fnet_a_kernel(x_ref, f1_ref, cw_ref, sw_ref, o_ref):
    a = _dot(f1_ref[...], x_ref[...])
    l1 = a.shape[0] // 2
    ac, asn = a[:l1], a[l1:]
    cw, sw = cw_ref[...], sw_ref[...]
    o_ref[0] = (ac * cw - asn * sw).astype(_BF16)
    o_ref[1] = (ac * sw + asn * cw).astype(_BF16)


def _fnet_b_kernel(a_ref, f2_ref, bc_ref, bs_ref, o_ref, *, scale, nk1, width):
    half = f2_ref.shape[0] // 2
    for j in range(nk1):
        rhs = jnp.concatenate([a_ref[0, j], a_ref[1, j]], axis=0)
        r = _dot(f2_ref[...], rhs)
        out = _dot(r[:half].astype(_BF16), bc_ref[...]) - _dot(r[half:].astype(_BF16), bs_ref[...])
        o_ref[:, j * width:(j + 1) * width] = (out * scale).astype(_BF16)


def _fnet_latent(zf):
    b, l, width = zf.shape
    l1, l2 = l // FFT_L2, FFT_L2
    f1, cw, sw, f2 = _dft_tables(l)
    bc, bs = _group_dft_tables(width)
    nc = FFT_T2_STEP * width
    cw = jnp.repeat(jnp.asarray(cw, _F32), width, axis=1)
    sw = jnp.repeat(jnp.asarray(sw, _F32), width, axis=1)
    a = pl.pallas_call(
        _fnet_a_kernel,
        grid=(l2 * width // nc, b),
        in_specs=[pl.BlockSpec((None, l1, nc), lambda j, bi: (bi, 0, j)),
                  pl.BlockSpec((2 * l1, l1), lambda j, bi: (0, 0)),
                  pl.BlockSpec((l1, nc), lambda j, bi: (0, j)),
                  pl.BlockSpec((l1, nc), lambda j, bi: (0, j))],
        out_specs=pl.BlockSpec((None, 2, l1, nc), lambda j, bi: (bi, 0, 0, j)),
        out_shape=jax.ShapeDtypeStruct((b, 2, l1, l2 * width), _BF16),
        compiler_params=_cparams("parallel", "parallel"),
        name="fnet_a",
    )(zf.reshape(b, l1, l2 * width), _table_bf16(f1), cw, sw)
    nk1 = FFT_K1_STEP
    out = pl.pallas_call(
        functools.partial(_fnet_b_kernel, scale=float((l * F_GROUP_DIM) ** -0.5), nk1=nk1, width=width),
        grid=(b, l1 // nk1),
        in_specs=[pl.BlockSpec((None, 2, nk1, l2, width), lambda bi, j: (bi, 0, j, 0, 0)),
                  pl.BlockSpec((2 * l2, 2 * l2), lambda bi, j: (0, 0)),
                  pl.BlockSpec((width, width), lambda bi, j: (0, 0)),
                  pl.BlockSpec((width, width), lambda bi, j: (0, 0))],
        out_specs=pl.BlockSpec((None, l2, nk1 * width), lambda bi, j: (bi, 0, j)),
        out_shape=jax.ShapeDtypeStruct((b, l2, l1 * width), _BF16),
        compiler_params=_cparams("parallel", "parallel"),
        name="fnet_b",
    )(a.reshape(b, 2, l1, l2, width), _table_bf16(f2), _table_bf16(bc), _table_bf16(bs))
    return out.reshape(b, l, width)


def _fnet_ctx_kernel(z_ref, cl_ref, sl_ref, bc_ref, bs_ref, o_ref, *, scale):
    z = z_ref[...]
    zc = _dot(z, bc_ref[...]).astype(_BF16)
    zs = _dot(z, bs_ref[...]).astype(_BF16)
    o_ref[...] = ((_dot(cl_ref[...], zc) - _dot(sl_ref[...], zs)) * scale).astype(_BF16)


def _fnet_ctx(zf):
    b, l, width = zf.shape
    al = 2 * np.pi * np.outer(np.arange(l), np.arange(l)) / l
    bc, bs = _group_dft_tables(width)
    full = lambda r, c: pl.BlockSpec((r, c), lambda bi: (0, 0))
    return pl.pallas_call(
        functools.partial(_fnet_ctx_kernel, scale=float((l * F_GROUP_DIM) ** -0.5)),
        grid=(b,),
        in_specs=[pl.BlockSpec((None, l, width), lambda bi: (bi, 0, 0)),
                  full(l, l), full(l, l), full(width, width), full(width, width)],
        out_specs=pl.BlockSpec((None, l, width), lambda bi: (bi, 0, 0)),
        out_shape=jax.ShapeDtypeStruct((b, l, width), _BF16),
        compiler_params=_cparams("parallel"),
        name="fnet_ctx",
    )(zf, _table_bf16(np.cos(al)), _table_bf16(np.sin(al)), _table_bf16(bc), _table_bf16(bs))


def _s5_disc_kernel(lr_ref, li_ref, ldt_ref, br_ref, bi_ref, lbr_ref, lbi_ref, bbr_ref, bbi_ref):
    lr = jnp.minimum(lr_ref[...], -1e-4)
    li = li_ref[...]
    dt = jnp.exp(ldt_ref[...])
    mag = jnp.exp(lr * dt)
    lbr = mag * jnp.cos(li * dt)
    lbi = mag * jnp.sin(li * dt)
    lbr_ref[...] = lbr
    lbi_ref[...] = lbi
    a, bb = lbr - 1.0, lbi
    den = lr * lr + li * li
    cr = (a * lr + bb * li) / den
    ci = (bb * lr - a * li) / den
    br, bi = br_ref[...], bi_ref[...]
    bbr_ref[...] = cr * br - ci * bi
    bbi_ref[...] = cr * bi + ci * br


def _s5_discretise(lam_re, lam_im, log_dt, b_re, b_im):
    _, g, p, n = b_re.shape
    rows = 2 * g * n
    ex = lambda t: jnp.broadcast_to(t[:, :, None, :], (2, g, n, p)).reshape(rows, p)
    ldt = jnp.broadcast_to(log_dt[:, :, None, None], (2, g, n, p)).reshape(rows, p)
    tb = lambda t: t.transpose(0, 1, 3, 2).reshape(rows, p)
    spec = pl.BlockSpec((rows, p), lambda: (0, 0))
    sds = jax.ShapeDtypeStruct((rows, p), _F32)
    lbr, lbi, bbr, bbi = pl.pallas_call(
        _s5_disc_kernel,
        in_specs=[spec] * 5, out_specs=[spec] * 4, out_shape=[sds] * 4,
        name="s5_discretise",
    )(ex(lam_re), ex(lam_im), ldt, tb(b_re), tb(b_im))
    shp = (2, g, n, p)
    return lbr.reshape(shp)[:, :, 0, :], lbi.reshape(shp)[:, :, 0, :], bbr.reshape(shp), bbi.reshape(shp)


def _s5_scan_kernel(ucf_ref, uxf_ref, ucr_ref, uxr_ref, perm_ref, permt_ref, bf_ref, br_ref, l_re_ref, l_im_ref,
                    cf_re_ref, cf_im_ref, cr_re_ref, cr_im_ref,
                    yfc_ref, yfx_ref, yrc_ref, yrx_ref,
                    d0a, d0b, d0c, d0d, d1a, d1b, d1c, d1d, h0a, h0b, h0c, h0d, h1a, h1b, h1c, h1d,
                    st_re, st_im, *, ts, nb, nc2, nst, sw):
    g = pl.program_id(0)
    d0, d1 = (d0a, d0b, d0c, d0d), (d1a, d1b, d1c, d1d)
    h0, h1 = (h0a, h0b, h0c, h0d), (h1a, h1b, h1c, h1d)

    @pl.when(g == 0)
    def _():
        for r in d0 + d1 + h0 + h1 + (st_re, st_im):
            r[...] = jnp.zeros_like(r)

    rt = SUBLANES
    ntile = ts // 2
    lower = lax.broadcasted_iota(jnp.int32, (rt, nst), 0) < nb
    is_ctx_in = g < nc2
    u_f = jnp.where(is_ctx_in, ucf_ref[...], uxf_ref[...])
    u_r = jnp.where(is_ctx_in, ucr_ref[...], uxr_ref[...])

    def time_batch_rows(u):
        stack = jnp.concatenate([u[:, b * sw:(b + 1) * sw] for b in range(nb)], axis=0)
        return _dot(perm_ref[...], stack).astype(_BF16)

    def drive(uf, ur, d):
        uf, ur = time_batch_rows(uf), time_batch_rows(ur)
        d[0][...] = _dot(uf, bf_ref[:, 0:nst])
        d[1][...] = _dot(uf, bf_ref[:, nst:2 * nst])
        d[2][...] = _dot(ur, br_ref[:, 0:nst])
        d[3][...] = _dot(ur, br_ref[:, nst:2 * nst])

    def scan(d, h):
        l_re, l_im = l_re_ref[...], l_im_ref[...]
        x_re, x_im = st_re[...], st_im[...]

        def step(x_re, x_im, s_re, s_im):
            return l_re * x_re - l_im * x_im + s_re, l_re * x_im + l_im * x_re + s_im

        for k in range(ntile // 2):
            f_tiles, r_tiles = [], []
            for m in (2 * k, 2 * k + 1):
                tf = slice(rt * m, rt * (m + 1))
                tr = slice(rt * (ntile - 1 - m), rt * (ntile - m))
                a_re, a_im, b_re, b_im = d[0][tf, :], d[1][tf, :], d[2][tr, :], d[3][tr, :]
                e_re, e_im = step(x_re, x_im, jnp.where(lower, a_re, b_re), jnp.where(lower, a_im, b_im))
                x_re, x_im = step(e_re, e_im, pltpu.roll(jnp.where(lower, b_re, a_re), nb, 0),
                                  pltpu.roll(jnp.where(lower, b_im, a_im), nb, 0))
                o_re, o_im = pltpu.roll(x_re, nb, 0), pltpu.roll(x_im, nb, 0)
                f_tiles.append((jnp.where(lower, e_re, o_re), jnp.where(lower, e_im, o_im)))
                r_tiles.append((jnp.where(lower, o_re, e_re), jnp.where(lower, o_im, e_im)))
            pf = slice(2 * rt * k, 2 * rt * (k + 1))
            pr = slice(2 * rt * (ntile // 2 - 1 - k), 2 * rt * (ntile // 2 - k))
            pack = lambda lo_t, hi_t: jnp.concatenate([lo_t, hi_t], axis=0).astype(_BF16)
            h[0][pf, :] = pack(f_tiles[0][0], f_tiles[1][0])
            h[1][pf, :] = pack(f_tiles[0][1], f_tiles[1][1])
            h[2][pr, :] = pack(r_tiles[1][0], r_tiles[0][0])
            h[3][pr, :] = pack(r_tiles[1][1], r_tiles[0][1])
        st_re[...] = x_re
        st_im[...] = x_im

    def readout(h):
        yf = (_dot(h[0][...], cf_re_ref[...]) - _dot(h[1][...], cf_im_ref[...])).astype(_BF16)
        yr = (_dot(h[2][...], cr_re_ref[...]) - _dot(h[3][...], cr_im_ref[...])).astype(_BF16)
        return _dot(permt_ref[...], yf).astype(_BF16), _dot(permt_ref[...], yr).astype(_BF16)

    lo, hi = slice(0, ts), slice(ts, 2 * ts)
    scan(d1, h1)
    drive(u_f[lo], u_r[hi], d0)
    yf0, yr0 = readout(h0)
    drive(u_f[hi], u_r[lo], d1)
    scan(d0, h0)
    yf1, yr1 = readout(h1)

    def store(yf_ref, yr_ref):
        for b in range(nb):
            cs = slice(b * sw, (b + 1) * sw)
            rows = slice(b * ts, (b + 1) * ts)
            yf_ref[lo, cs] = yf0[rows]
            yf_ref[hi, cs] = yf1[rows]
            yr_ref[hi, cs] = yr0[rows]
            yr_ref[lo, cs] = yr1[rows]

    pl.when(jnp.logical_and(g >= 1, g <= nc2))(lambda: store(yfc_ref, yrc_ref))
    pl.when(g > nc2)(lambda: store(yfx_ref, yrx_ref))


def _s5_scan(u_c, u_x, bf, br, l_re, l_im, cf_re, cf_im, cr_re, cr_im, nb, ts):
    sw = u_x.shape[1] // nb
    nst = l_re.shape[1]
    blk = 2 * ts
    nc2, nx2 = u_c.shape[0] // blk, u_x.shape[0] // blk
    nt2 = nc2 + nx2
    perm = np.zeros((ts * nb, ts * nb), np.float32)
    for t in range(ts):
        for b in range(nb):
            perm[t * nb + b, b * ts + t] = 1.0
    full = lambda a: pl.BlockSpec(a.shape, lambda g: (0, 0))
    spec = lambda f: pl.BlockSpec((blk, nb * sw), lambda g: (f(g), 0))
    clip = lambda v, n: jnp.clip(v, 0, n - 1)
    ins = [spec(lambda g: clip(g, nc2)), spec(lambda g: clip(g - nc2, nx2)),
           spec(lambda g: clip(nc2 - 1 - g, nc2)), spec(lambda g: clip(nt2 - 1 - g, nx2))]
    outs = [spec(lambda g: clip(g - 1, nc2)), spec(lambda g: clip(g - 1 - nc2, nx2)),
            spec(lambda g: clip(nc2 - g, nc2)), spec(lambda g: clip(nt2 - g, nx2))]
    perm_b, permt_b = _table_bf16(perm), _table_bf16(perm.T)
    sds = lambda a: jax.ShapeDtypeStruct(a.shape, _BF16)
    return pl.pallas_call(
        functools.partial(_s5_scan_kernel, ts=ts, nb=nb, nc2=nc2, nst=nst, sw=sw),
        grid=(nt2 + 1,),
        in_specs=ins + [full(perm_b), full(permt_b), full(bf), full(br), full(l_re), full(l_im),
                        full(cf_re), full(cf_im), full(cr_re), full(cr_im)],
        out_specs=outs,
        out_shape=[sds(u_c), sds(u_x), sds(u_c), sds(u_x)],
        scratch_shapes=([pltpu.VMEM((ts * nb, nst), _F32)] * 8 + [pltpu.VMEM((ts * nb, nst), _BF16)] * 8
                        + [pltpu.VMEM((SUBLANES, nst), _F32)] * 2),
        compiler_params=_cparams("arbitrary"),
        name="s5_scan",
    )(u_c, u_x, u_c, u_x, perm_b, permt_b, bf, br, l_re, l_im, cf_re, cf_im, cr_re, cr_im)


def _block_diag(t):
    g, r, c = t.shape
    eye = jnp.eye(g, dtype=t.dtype)
    return (t[:, :, None, :] * eye[:, None, :, None]).reshape(g * r, g * c)


def _s5_mix(zs_x, zs_c, nb, lam_re, lam_im, log_dt, b_re, b_im, c_re, c_im):
    assert 2 * nb == SUBLANES, "the scan state tile packs batch x direction on the sublanes"
    lbr, lbi, bbr, bbi = _s5_discretise(lam_re, lam_im, log_dt, b_re, b_im)
    nst = lbr.shape[1] * lbr.shape[2]
    drive = lambda d: jnp.concatenate([_block_diag(bbr[d]), _block_diag(bbi[d])], axis=1).astype(_BF16)
    rd = lambda t: _block_diag(t.transpose(0, 2, 1)).astype(_BF16)
    tile = lambda t: jnp.concatenate([jnp.broadcast_to(t[0].reshape(1, nst), (nb, nst)),
                                      jnp.broadcast_to(t[1].reshape(1, nst), (nb, nst))], axis=0)
    yf_c, yf_x, yr_c, yr_x = _s5_scan(zs_c, zs_x, drive(0), drive(1), tile(lbr), tile(lbi),
                                      rd(c_re[0]), rd(c_im[0]), rd(c_re[1]), rd(c_im[1]), nb, S5_TS)
    return (yf_x, yr_x), (yf_c, yr_c)


def _na_row_entries(rows):
    r = np.concatenate([np.arange(NA_QROWS), [NA_QROWS], rows - NA_QROWS + np.arange(NA_QROWS)])
    rs = np.clip(r - NA_KH // 2, 0, rows - NA_KH)
    idx = rs[:, None] + np.arange(NA_KH)[None, :] - r[:, None] + NA_KH - 1
    return tuple(tuple(int(v) for v in row) for row in idx)


def _na_table_kernel(rpb_ref, sel_ref, valid_ref, o_ref):
    t = jnp.dot(rpb_ref[...], sel_ref[...], precision=lax.Precision.HIGHEST, preferred_element_type=_F32)
    o_ref[...] = jnp.where(valid_ref[...] > 0.0, t, NEG_BIG)


def _na_column_tables(rpb):
    h, nr, ncol = rpb.shape
    w = np.arange(GRID_W)
    kc = np.arange(GRID_W)
    cs = np.clip(w - NA_KW // 2, 0, GRID_W - NA_KW)
    valid_c = (kc[None, :] >= cs[:, None]) & (kc[None, :] < cs[:, None] + NA_KW)
    idx_c = np.clip(kc[None, :] - w[:, None] + NA_KW - 1, 0, ncol - 1).reshape(-1)
    kpad = -(-ncol // SUBLANES) * SUBLANES
    sel = np.zeros((kpad, GRID_W * GRID_W), np.float32)
    sel[idx_c, np.arange(GRID_W * GRID_W)] = 1.0
    full = lambda r, c: pl.BlockSpec((r, c), lambda: (0, 0))
    out = pl.pallas_call(
        _na_table_kernel,
        in_specs=[full(h * nr, kpad), full(kpad, GRID_W * GRID_W), full(1, GRID_W * GRID_W)],
        out_specs=full(h * nr, GRID_W * GRID_W),
        out_shape=jax.ShapeDtypeStruct((h * nr, GRID_W * GRID_W), _F32),
        name="na_tables",
    )(jnp.pad(rpb.reshape(h * nr, ncol), ((0, 0), (0, kpad - ncol))), jnp.asarray(sel),
      jnp.asarray(valid_c.reshape(1, -1).astype(np.float32)))
    return out.reshape(h, nr, GRID_W, GRID_W)


def _na_kernel(q_ref, k_ref, v_ref, kc_ref, vc_ref, t_ref, o_ref, tab_ref, *, rows, nb, row_entries):
    blk = pl.program_id(2)
    dh = NA_HEAD_DIM
    win = NA_KH * GRID_W
    gw = GRID_W

    @pl.when(jnp.logical_and(pl.program_id(1) == 0, blk == 0))
    def _():
        for e, idx in enumerate(row_entries):
            for h in range(2):
                tab_ref[e, h * gw:(h + 1) * gw, :] = jnp.concatenate([t_ref[h, a] for a in idx], axis=1)

    q = q_ref[...]
    head0 = lax.broadcasted_iota(jnp.int32, q.shape, 1) < dh
    zero = jnp.zeros_like(q)
    q0, q1 = jnp.where(head0, q, zero), jnp.where(head0, zero, q)
    qq = jnp.concatenate([t[dr * gw:(dr + 1) * gw] for dr in range(NA_QROWS) for t in (q0, q1)], axis=0)
    starts, entries = [], []
    for dr in range(NA_QROWS):
        r = blk * NA_QROWS + dr
        rs = jnp.clip(r - NA_KH // 2, 0, rows - NA_KH)
        starts.append(pl.multiple_of(rs * gw, gw))
        entries.append(jnp.where(blk == 0, dr, jnp.where(blk == nb - 1, NA_QROWS + 1 + dr, NA_QROWS)))
    pair = lambda dr: slice(2 * dr * gw, 2 * (dr + 1) * gw)
    s_ctx = _dot_nt(qq, kc_ref[...])
    s_loc = jnp.concatenate(
        [_dot_nt(qq[pair(dr)], k_ref[pl.ds(starts[dr], win), :]) + tab_ref[entries[dr]]
         for dr in range(NA_QROWS)], axis=0)
    m = jnp.maximum(jnp.max(s_loc, axis=-1, keepdims=True), jnp.max(s_ctx, axis=-1, keepdims=True))
    p_loc = jnp.exp(s_loc - m)
    p_ctx = jnp.exp(s_ctx - m)
    den = jnp.sum(p_loc, axis=-1, keepdims=True) + jnp.sum(p_ctx, axis=-1, keepdims=True)
    p_loc = p_loc.astype(_BF16)
    o = jnp.concatenate(
        [_dot(p_loc[pair(dr)], v_ref[pl.ds(starts[dr], win), :]) for dr in range(NA_QROWS)], axis=0)
    o = (o + _dot(p_ctx.astype(_BF16), vc_ref[...])) / den
    h0_lanes = lax.broadcasted_iota(jnp.int32, (gw, q.shape[1]), 1) < dh
    for dr in range(NA_QROWS):
        o_ref[dr * gw:(dr + 1) * gw, :] = jnp.where(
            h0_lanes, o[2 * dr * gw:(2 * dr + 1) * gw], o[(2 * dr + 1) * gw:(2 * dr + 2) * gw]).astype(_BF16)


def _na_latent(q, k, v, kc, vc, rpb):
    b, l, aw = q.shape
    lc = kc.shape[1]
    rows = l // GRID_W
    assert rows % NA_QROWS == 0 and rows >= 2 * NA_QROWS
    nb = rows // NA_QROWS
    hp = aw // (2 * NA_HEAD_DIM)
    tq = NA_QROWS * GRID_W
    row_entries = _na_row_entries(rows)
    tables = _na_column_tables(rpb.astype(_F32))
    tables = tables.reshape((hp, 2) + tables.shape[1:])
    return pl.pallas_call(
        functools.partial(_na_kernel, rows=rows, nb=nb, row_entries=row_entries),
        grid=(hp, b, nb),
        in_specs=[pl.BlockSpec((None, tq, 2 * NA_HEAD_DIM), lambda p, bi, j: (bi, j, p)),
                  pl.BlockSpec((None, l, 2 * NA_HEAD_DIM), lambda p, bi, j: (bi, 0, p)),
                  pl.BlockSpec((None, l, 2 * NA_HEAD_DIM), lambda p, bi, j: (bi, 0, p)),
                  pl.BlockSpec((None, lc, 2 * NA_HEAD_DIM), lambda p, bi, j: (bi, 0, p)),
                  pl.BlockSpec((None, lc, 2 * NA_HEAD_DIM), lambda p, bi, j: (bi, 0, p)),
                  pl.BlockSpec((None,) + tables.shape[1:], lambda p, bi, j: (p, 0, 0, 0, 0))],
        out_specs=pl.BlockSpec((None, tq, 2 * NA_HEAD_DIM), lambda p, bi, j: (bi, j, p)),
        out_shape=jax.ShapeDtypeStruct((b, l, aw), _BF16),
        scratch_shapes=[pltpu.VMEM((len(row_entries), 2 * GRID_W, NA_KH * GRID_W), _F32)],
        compiler_params=_cparams("arbitrary", "arbitrary", "arbitrary"),
        name="na_latent",
    )(q, k, v, kc, vc, tables)


def _ctx_attn_kernel(q_ref, k_ref, v_ref, o_ref):
    dh = NA_HEAD_DIM
    outs = []
    for h in range(2):
        ls = slice(h * dh, (h + 1) * dh)
        s = _dot_nt(q_ref[:, ls], k_ref[:, ls])
        m = jnp.max(s, axis=-1, keepdims=True)
        p = jnp.exp(s - m)
        den = jnp.sum(p, axis=-1, keepdims=True)
        outs.append(_dot(p.astype(_BF16), v_ref[:, ls]) / den)
    o_ref[...] = jnp.concatenate(outs, axis=-1).astype(_BF16)


def _ctx_attention(q, k, v):
    b, lc, aw = q.shape
    hp = aw // (2 * NA_HEAD_DIM)
    spec = pl.BlockSpec((None, lc, 2 * NA_HEAD_DIM), lambda bi, p: (bi, 0, p))
    return pl.pallas_call(
        _ctx_attn_kernel,
        grid=(b, hp),
        in_specs=[spec] * 3, out_specs=spec,
        out_shape=jax.ShapeDtypeStruct((b, lc, aw), _BF16),
        compiler_params=_cparams("parallel", "parallel"),
        name="ctx_attention",
    )(q, k, v)


def _sigmoid(z):
    return 0.5 * jnp.tanh(0.5 * z) + 0.5


def _gelu_tanh(x):
    return 0.5 * x * (1.0 + jnp.tanh(math.sqrt(2.0 / math.pi) * (x + 0.044715 * (x * x * x))))


def _merge_kernel(x_ref, mod_ref, h_ref, fa_ref, u_ref, yf_ref, yr_ref, nc_ref, d_ref, wglu_ref,
                  wga_ref, wgb_ref, wgc_ref, wa_ref, wb_ref, wc_ref, wo_ref, o_ref):
    x = x_ref[...]
    h = h_ref[...]
    y = u_ref[...].astype(_F32) * d_ref[...] + yf_ref[...].astype(_F32) + yr_ref[...].astype(_F32)
    gl = _gelu_tanh(y)
    sb = gl * _sigmoid(_dot(gl.astype(_BF16), wglu_ref[...]))
    m = _sigmoid(_dot(h, wga_ref[...])) * _dot(fa_ref[...], wa_ref[...])
    m += _sigmoid(_dot(h, wgb_ref[...])) * _dot(sb.astype(_BF16), wb_ref[...])
    m += _sigmoid(_dot(h, wgc_ref[...])) * _dot(nc_ref[...], wc_ref[...])
    o_ref[...] = x + mod_ref[2:3, :] * _dot(m.astype(_BF16), wo_ref[...])


def _merge(x, mod, h, fa, u, yf, yr, nc, d_skip, w_glu, w_in, w_a, w_b, w_c, w_out, layer, mod_row, off_g, tm):
    b, l, d = x.shape
    assert off_g % d == 0
    mod_spec, mat = _layer_specs(layer, mod_row, d)
    gate = lambda k: pl.BlockSpec((None, d, d), lambda bi, i: (layer, 0, off_g // d + k))
    tok = lambda a: pl.BlockSpec((None, tm, a.shape[2]), lambda bi, i: (bi, i, 0))
    tmaj = lambda a: pl.BlockSpec((tm, a.shape[1] // b), lambda bi, i: (i, bi))
    return pl.pallas_call(
        _merge_kernel,
        grid=(b, l // tm),
        in_specs=[tok(x), mod_spec, tok(h), tok(fa), tmaj(u), tmaj(yf), tmaj(yr), tok(nc), mat(d_skip), mat(w_glu),
                  gate(0), gate(1), gate(2), mat(w_a), mat(w_b), mat(w_c), mat(w_out)],
        out_specs=tok(x),
        out_shape=jax.ShapeDtypeStruct((b, l, d), _F32),
        compiler_params=_cparams("parallel", "parallel"),
        name="merge",
    )(x, mod, h, fa, u, yf, yr, nc, d_skip, w_glu, w_in, w_in, w_in, w_a, w_b, w_c, w_out)


def _ffn_kernel(x_ref, mod_ref, g_ref, w1_ref, w2_ref, gf_ref, o_ref, *, final_norm, fchunk):
    x = x_ref[...]
    h = _norm_mod(x, g_ref[...], mod_ref[3:4, :], mod_ref[4:5, :]).astype(_BF16)
    dff = w1_ref.shape[1]
    acc = jnp.zeros_like(x)
    for c in range(dff // fchunk):
        cs = slice(c * fchunk, (c + 1) * fchunk)
        t = jnp.maximum(_dot(h, w1_ref[:, cs]), 0.0)
        acc += _dot((t * t).astype(_BF16), w2_ref[cs, :])
    x2 = x + mod_ref[5:6, :] * acc
    if final_norm:
        ms = jnp.mean(x2 * x2, axis=-1, keepdims=True)
        x2 = x2 * lax.rsqrt(ms + RMS_EPS) * gf_ref[...]
    o_ref[...] = x2


def _ffn(x, mod, g, w1, w2, g_final, final_norm, layer, mod_row, tm):
    b, l, d = x.shape
    mod_spec, mat = _layer_specs(layer, mod_row, d)
    tok = pl.BlockSpec((None, tm, d), lambda bi, i: (bi, i, 0))
    full = lambda a: pl.BlockSpec(a.shape, lambda bi, i: (0, 0))
    return pl.pallas_call(
        functools.partial(_ffn_kernel, final_norm=final_norm, fchunk=FFN_CHUNK),
        grid=(b, l // tm),
        in_specs=[tok, mod_spec, mat(g), mat(w1), mat(w2), full(g_final)],
        out_specs=tok,
        out_shape=jax.ShapeDtypeStruct((b, l, d), _F32),
        compiler_params=_cparams("parallel", "parallel"),
        name="ffn",
    )(x, mod, g, w1, w2, g_final)


def kernel(x, c, ctx, c_ctx, w_mod, b_mod, g_norm1, g_norm2, w_in, w_br_a, w_br_b, w_br_c, w_out, s5_lam_re, s5_lam_im, s5_log_dt, s5_b_re, s5_b_im, s5_c_re, s5_c_im, s5_d, s5_w_glu, na_rpb, w_ff1, w_ff2, g_final):
    b, l, d = x.shape
    lc = ctx.shape[1]
    depth = w_mod.shape[0]
    fw = w_br_a.shape[1]
    sw = w_br_b.shape[1]
    aw = w_br_c.shape[1]
    off_g = fw + sw + 3 * aw
    tm = TOKEN_TILE
    tmc = lc

    c8 = jnp.concatenate([c, c_ctx[None, :], jnp.zeros((SUBLANES - b - 1, d), _F32)], axis=0)
    mod = _modulation(c8, w_mod, b_mod).reshape(depth, SUBLANES, 6, d)
    gf = g_final.reshape(1, d)
    g1 = g_norm1.reshape(depth, 1, d)
    g2 = g_norm2.reshape(depth, 1, d)
    d_skip = s5_d.reshape(depth, 1, sw)
    bf = lambda t: t.astype(_BF16)
    w_in_b, w_a, w_b, w_c, w_o, w_glu, w1, w2 = (bf(w_in), bf(w_br_a), bf(w_br_b), bf(w_br_c), bf(w_out),
                                                 bf(s5_w_glu), bf(w_ff1), bf(w_ff2))
    widths = (fw, sw, aw)

    for i in range(depth):
        need_ctx_out = i < depth - 1
        h_x, zf_x, zs_x, q_x, k_x, v_x = _inproj(x, mod, g1, w_in_b, i, None, widths, tm)
        h_c, zf_c, zs_c, q_c, k_c, v_c = _inproj(ctx, mod, g1, w_in_b, i, b, widths, tmc)

        fa_x = _fnet_latent(zf_x)
        (yf_x, yr_x), (yf_c, yr_c) = _s5_mix(zs_x, zs_c, b, s5_lam_re[i], s5_lam_im[i], s5_log_dt[i],
                                              s5_b_re[i], s5_b_im[i], s5_c_re[i], s5_c_im[i])
        nc_x = _na_latent(q_x, k_x, v_x, k_c, v_c, na_rpb[i])

        branch_w = (d_skip, w_glu, w_in_b, w_a, w_b, w_c, w_o)
        x = _merge(x, mod, h_x, fa_x, zs_x, yf_x, yr_x, nc_x, *branch_w, i, None, off_g, tm)
        x = _ffn(x, mod, g2, w1, w2, gf, not need_ctx_out, i, None, tm)

        if need_ctx_out:
            fa_c = _fnet_ctx(zf_c)
            nc_c = _ctx_attention(q_c, k_c, v_c)
            ctx = _merge(ctx, mod, h_c, fa_c, zs_c, yf_c, yr_c, nc_c, *branch_w, i, b, off_g, tmc)
            ctx = _ffn(ctx, mod, g2, w1, w2, gf, False, i, b, tmc)
    return x
```

```python
import functools
import math

import numpy as np
import jax
import jax.numpy as jnp
from jax import lax
from jax.experimental import pallas as pl
from jax.experimental.pallas import tpu as pltpu

_F32 = jnp.float32
_BF16 = jnp.bfloat16

RMS_EPS = 1e-6
GRID_W = 64
F_GROUP_DIM = 64
NA_HEAD_DIM = 64
NA_KH = 8
NA_KW = 16
NA_QROWS = 16
NEG_BIG = -1e30
FFT_L2 = 128
FFT_T2_STEP = 16
FFT_K1_STEP = 8
S5_TS = 64
TOKEN_TILE = 1024
FFN_CHUNK = 1024
MOD_TN = 1024

SUBLANES = 8
VMEM_LIMIT_BYTES = 56 * 1024 * 1024


def _cparams(*sem):
    return pltpu.CompilerParams(dimension_semantics=sem, vmem_limit_bytes=VMEM_LIMIT_BYTES)


def _dot(a, b):
    return jnp.dot(a, b, preferred_element_type=_F32)


def _table_bf16(t):
    return jnp.asarray(t, _F32).astype(_BF16)


def _dot_nt(a, b):
    return lax.dot_general(a, b, (((1,), (1,)), ((), ())), preferred_element_type=_F32)


def _norm_mod(x, g, shift, scale):
    ms = jnp.mean(x * x, axis=-1, keepdims=True)
    y = x * lax.rsqrt(ms + RMS_EPS) * g
    return y * (1.0 + scale) + shift


def _mod_kernel(c_ref, w_ref, b_ref, o_ref):
    c = c_ref[...]
    s = c * jax.nn.sigmoid(c)
    o_ref[...] = jnp.dot(s, w_ref[...], precision=lax.Precision.HIGHEST,
                         preferred_element_type=_F32) + b_ref[...]


def _modulation(c8, w_mod, b_mod):
    depth, d, n = w_mod.shape
    tn = MOD_TN
    rows = c8.shape[0]
    return pl.pallas_call(
        _mod_kernel,
        grid=(depth, n // tn),
        in_specs=[pl.BlockSpec((rows, d), lambda l, j: (0, 0)),
                  pl.BlockSpec((None, d, tn), lambda l, j: (l, 0, j)),
                  pl.BlockSpec((None, 1, tn), lambda l, j: (l, 0, j))],
        out_specs=pl.BlockSpec((None, rows, tn), lambda l, j: (l, 0, j)),
        out_shape=jax.ShapeDtypeStruct((depth, rows, n), _F32),
        compiler_params=_cparams("parallel", "parallel"),
        name="modulation",
    )(c8, w_mod, b_mod.reshape(depth, 1, n))


def _inproj_kernel(x_ref, mod_ref, g_ref, w_ref, h_ref, zf_ref, zs_ref, q_ref, k_ref, v_ref, *, widths, q_scale):
    h = _norm_mod(x_ref[...], g_ref[...], mod_ref[0:1, :], mod_ref[1:2, :]).astype(_BF16)
    h_ref[...] = h
    fw, sw, aw = widths
    o = 0
    zf_ref[...] = _dot(h, w_ref[:, o:o + fw]).astype(_BF16)
    o += fw
    zs_ref[...] = _dot(h, w_ref[:, o:o + sw]).astype(_BF16)
    o += sw
    q_ref[...] = (_dot(h, w_ref[:, o:o + aw]) * q_scale).astype(_BF16)
    o += aw
    k_ref[...] = _dot(h, w_ref[:, o:o + aw]).astype(_BF16)
    o += aw
    v_ref[...] = _dot(h, w_ref[:, o:o + aw]).astype(_BF16)


def _layer_specs(layer, mod_row, d):
    mod = pl.BlockSpec((None, None, 6, d), lambda bi, i: (layer, bi if mod_row is None else mod_row, 0, 0))
    mat = lambda a: pl.BlockSpec((None,) + a.shape[1:], lambda bi, i: (layer, 0, 0))
    return mod, mat


def _inproj(x, mod, g, w, layer, mod_row, widths, tm):
    b, l, d = x.shape
    fw, sw, aw = widths
    n = fw + sw + 3 * aw
    mod_spec, mat = _layer_specs(layer, mod_row, d)
    tok = lambda width: pl.BlockSpec((None, tm, width), lambda bi, i: (bi, i, 0))
    return pl.pallas_call(
        functools.partial(_inproj_kernel, widths=widths, q_scale=NA_HEAD_DIM ** -0.5),
        grid=(b, l // tm),
        in_specs=[tok(d), mod_spec, mat(g), pl.BlockSpec((None, d, n), lambda bi, i: (layer, 0, 0))],
        out_specs=[tok(d), tok(fw), pl.BlockSpec((tm, sw), lambda bi, i: (i, bi)), tok(aw), tok(aw), tok(aw)],
        out_shape=[jax.ShapeDtypeStruct((b, l, d), _BF16), jax.ShapeDtypeStruct((b, l, fw), _BF16), jax.ShapeDtypeStruct((l, b * sw), _BF16),
                   jax.ShapeDtypeStruct((b, l, aw), _BF16), jax.ShapeDtypeStruct((b, l, aw), _BF16),
                   jax.ShapeDtypeStruct((b, l, aw), _BF16)],
        compiler_params=_cparams("parallel", "parallel"),
        name="inproj",
    )(x, mod, g, w)


def _dft_tables(l):
    l1, l2 = l // FFT_L2, FFT_L2
    a1 = 2 * np.pi * np.outer(np.arange(l1), np.arange(l1)) / l1
    f1 = np.concatenate([np.cos(a1), np.sin(a1)], axis=0)
    at = 2 * np.pi * np.outer(np.arange(l1), np.arange(l2)) / l
    a2 = 2 * np.pi * np.outer(np.arange(l2), np.arange(l2)) / l2
    c2, s2 = np.cos(a2), np.sin(a2)
    f2 = np.block([[c2, -s2], [s2, c2]])
    return f1, np.cos(at), np.sin(at), f2


def _group_dft_tables(width):
    ac = 2 * np.pi * np.outer(np.arange(F_GROUP_DIM), np.arange(F_GROUP_DIM)) / F_GROUP_DIM
    eye = np.eye(width // F_GROUP_DIM)
    return np.kron(eye, np.cos(ac)), np.kron(eye, np.sin(ac))


def _fnet_a_kernel(x_ref, f1_ref, cw_ref, sw_ref, o_ref):
    a = _dot(f1_ref[...], x_ref[...])
    l1 = a.shape[0] // 2
    ac, asn = a[:l1], a[l1:]
    cw, sw = cw_ref[...], sw_ref[...]
    o_ref[0] = (ac * cw - asn * sw).astype(_BF16)
    o_ref[1] = (ac * sw + asn * cw).astype(_BF16)


def _fnet_b_kernel(a_ref, f2_ref, bc_ref, bs_ref, o_ref, *, scale, nk1, width):
    half = f2_ref.shape[0] // 2
    for j in range(nk1):
        rhs = jnp.concatenate([a_ref[0, j], a_ref[1, j]], axis=0)
        r = _dot(f2_ref[...], rhs)
        out = _dot(r[:half].astype(_BF16), bc_ref[...]) - _dot(r[half:].astype(_BF16), bs_ref[...])
        o_ref[:, j * width:(j + 1) * width] = (out * scale).astype(_BF16)


def _fnet_latent(zf):
    b, l, width = zf.shape
    l1, l2 = l // FFT_L2, FFT_L2
    f1, cw, sw, f2 = _dft_tables(l)
    bc, bs = _group_dft_tables(width)
    nc = FFT_T2_STEP * width
    cw = jnp.repeat(jnp.asarray(cw, _F32), width, axis=1)
    sw = jnp.repeat(jnp.asarray(sw, _F32), width, axis=1)
    a = pl.pallas_call(
        _fnet_a_kernel,
        grid=(l2 * width // nc, b),
        in_specs=[pl.BlockSpec((None, l1, nc), lambda j, bi: (bi, 0, j)),
                  pl.BlockSpec((2 * l1, l1), lambda j, bi: (0, 0)),
                  pl.BlockSpec((l1, nc), lambda j, bi: (0, j)),
                  pl.BlockSpec((l1, nc), lambda j, bi: (0, j))],
        out_specs=pl.BlockSpec((None, 2, l1, nc), lambda j, bi: (bi, 0, 0, j)),
        out_shape=jax.ShapeDtypeStruct((b, 2, l1, l2 * width), _BF16),
        compiler_params=_cparams("parallel", "parallel"),
        name="fnet_a",
    )(zf.reshape(b, l1, l2 * width), _table_bf16(f1), cw, sw)
    nk1 = FFT_K1_STEP
    out = pl.pallas_call(
        functools.partial(_fnet_b_kernel, scale=float((l * F_GROUP_DIM) ** -0.5), nk1=nk1, width=width),
        grid=(b, l1 // nk1),
        in_specs=[pl.BlockSpec((None, 2, nk1, l2, width), lambda bi, j: (bi, 0, j, 0, 0)),
                  pl.BlockSpec((2 * l2, 2 * l2), lambda bi, j: (0, 0)),
                  pl.BlockSpec((width, width), lambda bi, j: (0, 0)),
                  pl.BlockSpec((width, width), lambda bi, j: (0, 0))],
        out_specs=pl.BlockSpec((None, l2, nk1 * width), lambda bi, j: (bi, 0, j)),
        out_shape=jax.ShapeDtypeStruct((b, l2, l1 * width), _BF16),
        compiler_params=_cparams("parallel", "parallel"),
        name="fnet_b",
    )(a.reshape(b, 2, l1, l2, width), _table_bf16(f2), _table_bf16(bc), _table_bf16(bs))
    return out.reshape(b, l, width)


def _fnet_ctx_kernel(z_ref, cl_ref, sl_ref, bc_ref, bs_ref, o_ref, *, scale):
    z = z_ref[...]
    zc = _dot(z, bc_ref[...]).astype(_BF16)
    zs = _dot(z, bs_ref[...]).astype(_BF16)
    o_ref[...] = ((_dot(cl_ref[...], zc) - _dot(sl_ref[...], zs)) * scale).astype(_BF16)


def _fnet_ctx(zf):
    b, l, width = zf.shape
    al = 2 * np.pi * np.outer(np.arange(l), np.arange(l)) / l
    bc, bs = _group_dft_tables(width)
    full = lambda r, c: pl.BlockSpec((r, c), lambda bi: (0, 0))
    return pl.pallas_call(
        functools.partial(_fnet_ctx_kernel, scale=float((l * F_GROUP_DIM) ** -0.5)),
        grid=(b,),
        in_specs=[pl.BlockSpec((None, l, width), lambda bi: (bi, 0, 0)),
                  full(l, l), full(l, l), full(width, width), full(width, width)],
        out_specs=pl.BlockSpec((None, l, width), lambda bi: (bi, 0, 0)),
        out_shape=jax.ShapeDtypeStruct((b, l, width), _BF16),
        compiler_params=_cparams("parallel"),
        name="fnet_ctx",
    )(zf, _table_bf16(np.cos(al)), _table_bf16(np.sin(al)), _table_bf16(bc), _table_bf16(bs))


def _s5_disc_kernel(lr_ref, li_ref, ldt_ref, br_ref, bi_ref, lbr_ref, lbi_ref, bbr_ref, bbi_ref):
    lr = jnp.minimum(lr_ref[...], -1e-4)
    li = li_ref[...]
    dt = jnp.exp(ldt_ref[...])
    mag = jnp.exp(lr * dt)
    lbr = mag * jnp.cos(li * dt)
    lbi = mag * jnp.sin(li * dt)
    lbr_ref[...] = lbr
    lbi_ref[...] = lbi
    a, bb = lbr - 1.0, lbi
    den = lr * lr + li * li
    cr = (a * lr + bb * li) / den
    ci = (bb * lr - a * li) / den
    br, bi = br_ref[...], bi_ref[...]
    bbr_ref[...] = cr * br - ci * bi
    bbi_ref[...] = cr * bi + ci * br


def _s5_discretise(lam_re, lam_im, log_dt, b_re, b_im):
    _, g, p, n = b_re.shape
    rows = 2 * g * n
    ex = lambda t: jnp.broadcast_to(t[:, :, None, :], (2, g, n, p)).reshape(rows, p)
    ldt = jnp.broadcast_to(log_dt[:, :, None, None], (2, g, n, p)).reshape(rows, p)
    tb = lambda t: t.transpose(0, 1, 3, 2).reshape(rows, p)
    spec = pl.BlockSpec((rows, p), lambda: (0, 0))
    sds = jax.ShapeDtypeStruct((rows, p), _F32)
    lbr, lbi, bbr, bbi = pl.pallas_call(
        _s5_disc_kernel,
        in_specs=[spec] * 5, out_specs=[spec] * 4, out_shape=[sds] * 4,
        name="s5_discretise",
    )(ex(lam_re), ex(lam_im), ldt, tb(b_re), tb(b_im))
    shp = (2, g, n, p)
    return lbr.reshape(shp)[:, :, 0, :], lbi.reshape(shp)[:, :, 0, :], bbr.reshape(shp), bbi.reshape(shp)


def _s5_scan_kernel(ucf_ref, uxf_ref, ucr_ref, uxr_ref, perm_ref, permt_ref, bf_ref, br_ref, l_re_ref, l_im_ref,
                    cf_re_ref, cf_im_ref, cr_re_ref, cr_im_ref,
                    yfc_ref, yfx_ref, yrc_ref, yrx_ref,
                    d0a, d0b, d0c, d0d, d1a, d1b, d1c, d1d, h0a, h0b, h0c, h0d, h1a, h1b, h1c, h1d,
                    st_re, st_im, *, ts, nb, nc2, nst, sw):
    g = pl.program_id(0)
    d0, d1 = (d0a, d0b, d0c, d0d), (d1a, d1b, d1c, d1d)
    h0, h1 = (h0a, h0b, h0c, h0d), (h1a, h1b, h1c, h1d)

    @pl.when(g == 0)
    def _():
        for r in d0 + d1 + h0 + h1 + (st_re, st_im):
            r[...] = jnp.zeros_like(r)

    rt = SUBLANES
    ntile = ts // 2
    lower = lax.broadcasted_iota(jnp.int32, (rt, nst), 0) < nb
    is_ctx_in = g < nc2
    u_f = jnp.where(is_ctx_in, ucf_ref[...], uxf_ref[...])
    u_r = jnp.where(is_ctx_in, ucr_ref[...], uxr_ref[...])

    def time_batch_rows(u):
        stack = jnp.concatenate([u[:, b * sw:(b + 1) * sw] for b in range(nb)], axis=0)
        return _dot(perm_ref[...], stack).astype(_BF16)

    def drive(uf, ur, d):
        uf, ur = time_batch_rows(uf), time_batch_rows(ur)
        d[0][...] = _dot(uf, bf_ref[:, 0:nst])
        d[1][...] = _dot(uf, bf_ref[:, nst:2 * nst])
        d[2][...] = _dot(ur, br_ref[:, 0:nst])
        d[3][...] = _dot(ur, br_ref[:, nst:2 * nst])

    def scan(d, h):
        l_re, l_im = l_re_ref[...], l_im_ref[...]
        x_re, x_im = st_re[...], st_im[...]

        def step(x_re, x_im, s_re, s_im):
            return l_re * x_re - l_im * x_im + s_re, l_re * x_im + l_im * x_re + s_im

        for k in range(ntile // 2):
            f_tiles, r_tiles = [], []
            for m in (2 * k, 2 * k + 1):
                tf = slice(rt * m, rt * (m + 1))
                tr = slice(rt * (ntile - 1 - m), rt * (ntile - m))
                a_re, a_im, b_re, b_im = d[0][tf, :], d[1][tf, :], d[2][tr, :], d[3][tr, :]
                e_re, e_im = step(x_re, x_im, jnp.where(lower, a_re, b_re), jnp.where(lower, a_im, b_im))
                x_re, x_im = step(e_re, e_im, pltpu.roll(jnp.where(lower, b_re, a_re), nb, 0),
                                  pltpu.roll(jnp.where(lower, b_im, a_im), nb, 0))
                o_re, o_im = pltpu.roll(x_re, nb, 0), pltpu.roll(x_im, nb, 0)
                f_tiles.append((jnp.where(lower, e_re, o_re), jnp.where(lower, e_im, o_im)))
                r_tiles.append((jnp.where(lower, o_re, e_re), jnp.where(lower, o_im, e_im)))
            pf = slice(2 * rt * k, 2 * rt * (k + 1))
            pr = slice(2 * rt * (ntile // 2 - 1 - k), 2 * rt * (ntile // 2 - k))
            pack = lambda lo_t, hi_t: jnp.concatenate([lo_t, hi_t], axis=0).astype(_BF16)
            h[0][pf, :] = pack(f_tiles[0][0], f_tiles[1][0])
            h[1][pf, :] = pack(f_tiles[0][1], f_tiles[1][1])
            h[2][pr, :] = pack(r_tiles[1][0], r_tiles[0][0])
            h[3][pr, :] = pack(r_tiles[1][1], r_tiles[0][1])
        st_re[...] = x_re
        st_im[...] = x_im

    def readout(h):
        yf = (_dot(h[0][...], cf_re_ref[...]) - _dot(h[1][...], cf_im_ref[...])).astype(_BF16)
        yr = (_dot(h[2][...], cr_re_ref[...]) - _dot(h[3][...], cr_im_ref[...])).astype(_BF16)
        return _dot(permt_ref[...], yf).astype(_BF16), _dot(permt_ref[...], yr).astype(_BF16)

    lo, hi = slice(0, ts), slice(ts, 2 * ts)
    scan(d1, h1)
    drive(u_f[lo], u_r[hi], d0)
    yf0, yr0 = readout(h0)
    drive(u_f[hi], u_r[lo], d1)
    scan(d0, h0)
    yf1, yr1 = readout(h1)

    def store(yf_ref, yr_ref):
        for b in range(nb):
            cs = slice(b * sw, (b + 1) * sw)
            rows = slice(b * ts, (b + 1) * ts)
            yf_ref[lo, cs] = yf0[rows]
            yf_ref[hi, cs] = yf1[rows]
            yr_ref[hi, cs] = yr0[rows]
            yr_ref[lo, cs] = yr1[rows]

    pl.when(jnp.logical_and(g >= 1, g <= nc2))(lambda: store(yfc_ref, yrc_ref))
    pl.when(g > nc2)(lambda: store(yfx_ref, yrx_ref))


def _s5_scan(u_c, u_x, bf, br, l_re, l_im, cf_re, cf_im, cr_re, cr_im, nb, ts):
    sw = u_x.shape[1] // nb
    nst = l_re.shape[1]
    blk = 2 * ts
    nc2, nx2 = u_c.shape[0] // blk, u_x.shape[0] // blk
    nt2 = nc2 + nx2
    perm = np.zeros((ts * nb, ts * nb), np.float32)
    for t in range(ts):
        for b in range(nb):
            perm[t * nb + b, b * ts + t] = 1.0
    full = lambda a: pl.BlockSpec(a.shape, lambda g: (0, 0))
    spec = lambda f: pl.BlockSpec((blk, nb * sw), lambda g: (f(g), 0))
    clip = lambda v, n: jnp.clip(v, 0, n - 1)
    ins = [spec(lambda g: clip(g, nc2)), spec(lambda g: clip(g - nc2, nx2)),
           spec(lambda g: clip(nc2 - 1 - g, nc2)), spec(lambda g: clip(nt2 - 1 - g, nx2))]
    outs = [spec(lambda g: clip(g - 1, nc2)), spec(lambda g: clip(g - 1 - nc2, nx2)),
            spec(lambda g: clip(nc2 - g, nc2)), spec(lambda g: clip(nt2 - g, nx2))]
    perm_b, permt_b = _table_bf16(perm), _table_bf16(perm.T)
    sds = lambda a: jax.ShapeDtypeStruct(a.shape, _BF16)
    return pl.pallas_call(
        functools.partial(_s5_scan_kernel, ts=ts, nb=nb, nc2=nc2, nst=nst, sw=sw),
        grid=(nt2 + 1,),
        in_specs=ins + [full(perm_b), full(permt_b), full(bf), full(br), full(l_re), full(l_im),
                        full(cf_re), full(cf_im), full(cr_re), full(cr_im)],
        out_specs=outs,
        out_shape=[sds(u_c), sds(u_x), sds(u_c), sds(u_x)],
        scratch_shapes=([pltpu.VMEM((ts * nb, nst), _F32)] * 8 + [pltpu.VMEM((ts * nb, nst), _BF16)] * 8
                        + [pltpu.VMEM((SUBLANES, nst), _F32)] * 2),
        compiler_params=_cparams("arbitrary"),
        name="s5_scan",
    )(u_c, u_x, u_c, u_x, perm_b, permt_b, bf, br, l_re, l_im, cf_re, cf_im, cr_re, cr_im)


def _block_diag(t):
    g, r, c = t.shape
    eye = jnp.eye(g, dtype=t.dtype)
    return (t[:, :, None, :] * eye[:, None, :, None]).reshape(g * r, g * c)


def _s5_mix(zs_x, zs_c, nb, lam_re, lam_im, log_dt, b_re, b_im, c_re, c_im):
    assert 2 * nb == SUBLANES, "the scan state tile packs batch x direction on the sublanes"
    lbr, lbi, bbr, bbi = _s5_discretise(lam_re, lam_im, log_dt, b_re, b_im)
    nst = lbr.shape[1] * lbr.shape[2]
    drive = lambda d: jnp.concatenate([_block_diag(bbr[d]), _block_diag(bbi[d])], axis=1).astype(_BF16)
    rd = lambda t: _block_diag(t.transpose(0, 2, 1)).astype(_BF16)
    tile = lambda t: jnp.concatenate([jnp.broadcast_to(t[0].reshape(1, nst), (nb, nst)),
                                      jnp.broadcast_to(t[1].reshape(1, nst), (nb, nst))], axis=0)
    yf_c, yf_x, yr_c, yr_x = _s5_scan(zs_c, zs_x, drive(0), drive(1), tile(lbr), tile(lbi),
                                      rd(c_re[0]), rd(c_im[0]), rd(c_re[1]), rd(c_im[1]), nb, S5_TS)
    return (yf_x, yr_x), (yf_c, yr_c)


def _na_row_entries(rows):
    r = np.concatenate([np.arange(NA_QROWS), [NA_QROWS], rows - NA_QROWS + np.arange(NA_QROWS)])
    rs = np.clip(r - NA_KH // 2, 0, rows - NA_KH)
    idx = rs[:, None] + np.arange(NA_KH)[None, :] - r[:, None] + NA_KH - 1
    return tuple(tuple(int(v) for v in row) for row in idx)


def _na_table_kernel(rpb_ref, sel_ref, valid_ref, o_ref):
    t = jnp.dot(rpb_ref[...], sel_ref[...], precision=lax.Precision.HIGHEST, preferred_element_type=_F32)
    o_ref[...] = jnp.where(valid_ref[...] > 0.0, t, NEG_BIG)


def _na_column_tables(rpb):
    h, nr, ncol = rpb.shape
    w = np.arange(GRID_W)
    kc = np.arange(GRID_W)
    cs = np.clip(w - NA_KW // 2, 0, GRID_W - NA_KW)
    valid_c = (kc[None, :] >= cs[:, None]) & (kc[None, :] < cs[:, None] + NA_KW)
    idx_c = np.clip(kc[None, :] - w[:, None] + NA_KW - 1, 0, ncol - 1).reshape(-1)
    kpad = -(-ncol // SUBLANES) * SUBLANES
    sel = np.zeros((kpad, GRID_W * GRID_W), np.float32)
    sel[idx_c, np.arange(GRID_W * GRID_W)] = 1.0
    full = lambda r, c: pl.BlockSpec((r, c), lambda: (0, 0))
    out = pl.pallas_call(
        _na_table_kernel,
        in_specs=[full(h * nr, kpad), full(kpad, GRID_W * GRID_W), full(1, GRID_W * GRID_W)],
        out_specs=full(h * nr, GRID_W * GRID_W),
        out_shape=jax.ShapeDtypeStruct((h * nr, GRID_W * GRID_W), _F32),
        name="na_tables",
    )(jnp.pad(rpb.reshape(h * nr, ncol), ((0, 0), (0, kpad - ncol))), jnp.asarray(sel),
      jnp.asarray(valid_c.reshape(1, -1).astype(np.float32)))
    return out.reshape(h, nr, GRID_W, GRID_W)


def _na_kernel(q_ref, k_ref, v_ref, kc_ref, vc_ref, t_ref, o_ref, tab_ref, *, rows, nb, row_entries):
    blk = pl.program_id(2)
    dh = NA_HEAD_DIM
    win = NA_KH * GRID_W
    gw = GRID_W

    @pl.when(jnp.logical_and(pl.program_id(1) == 0, blk == 0))
    def _():
        for e, idx in enumerate(row_entries):
            for h in range(2):
                tab_ref[e, h * gw:(h + 1) * gw, :] = jnp.concatenate([t_ref[h, a] for a in idx], axis=1)

    q = q_ref[...]
    head0 = lax.broadcasted_iota(jnp.int32, q.shape, 1) < dh
    zero = jnp.zeros_like(q)
    q0, q1 = jnp.where(head0, q, zero), jnp.where(head0, zero, q)
    qq = jnp.concatenate([t[dr * gw:(dr + 1) * gw] for dr in range(NA_QROWS) for t in (q0, q1)], axis=0)
    starts, entries = [], []
    for dr in range(NA_QROWS):
        r = blk * NA_QROWS + dr
        rs = jnp.clip(r - NA_KH // 2, 0, rows - NA_KH)
        starts.append(pl.multiple_of(rs * gw, gw))
        entries.append(jnp.where(blk == 0, dr, jnp.where(blk == nb - 1, NA_QROWS + 1 + dr, NA_QROWS)))
    pair = lambda dr: slice(2 * dr * gw, 2 * (dr + 1) * gw)
    s_ctx = _dot_nt(qq, kc_ref[...])
    s_loc = jnp.concatenate(
        [_dot_nt(qq[pair(dr)], k_ref[pl.ds(starts[dr], win), :]) + tab_ref[entries[dr]]
         for dr in range(NA_QROWS)], axis=0)
    m = jnp.maximum(jnp.max(s_loc, axis=-1, keepdims=True), jnp.max(s_ctx, axis=-1, keepdims=True))
    p_loc = jnp.exp(s_loc - m)
    p_ctx = jnp.exp(s_ctx - m)
    den = jnp.sum(p_loc, axis=-1, keepdims=True) + jnp.sum(p_ctx, axis=-1, keepdims=True)
    p_loc = p_loc.astype(_BF16)
    o = jnp.concatenate(
        [_dot(p_loc[pair(dr)], v_ref[pl.ds(starts[dr], win), :]) for dr in range(NA_QROWS)], axis=0)
    o = (o + _dot(p_ctx.astype(_BF16), vc_ref[...])) / den
    h0_lanes = lax.broadcasted_iota(jnp.int32, (gw, q.shape[1]), 1) < dh
    for dr in range(NA_QROWS):
        o_ref[dr * gw:(dr + 1) * gw, :] = jnp.where(
            h0_lanes, o[2 * dr * gw:(2 * dr + 1) * gw], o[(2 * dr + 1) * gw:(2 * dr + 2) * gw]).astype(_BF16)


def _na_latent(q, k, v, kc, vc, rpb):
    b, l, aw = q.shape
    lc = kc.shape[1]
    rows = l // GRID_W
    assert rows % NA_QROWS == 0 and rows >= 2 * NA_QROWS
    nb = rows // NA_QROWS
    hp = aw // (2 * NA_HEAD_DIM)
    tq = NA_QROWS * GRID_W
    row_entries = _na_row_entries(rows)
    tables = _na_column_tables(rpb.astype(_F32))
    tables = tables.reshape((hp, 2) + tables.shape[1:])
    return pl.pallas_call(
        functools.partial(_na_kernel, rows=rows, nb=nb, row_entries=row_entries),
        grid=(hp, b, nb),
        in_specs=[pl.BlockSpec((None, tq, 2 * NA_HEAD_DIM), lambda p, bi, j: (bi, j, p)),
                  pl.BlockSpec((None, l, 2 * NA_HEAD_DIM), lambda p, bi, j: (bi, 0, p)),
                  pl.BlockSpec((None, l, 2 * NA_HEAD_DIM), lambda p, bi, j: (bi, 0, p)),
                  pl.BlockSpec((None, lc, 2 * NA_HEAD_DIM), lambda p, bi, j: (bi, 0, p)),
                  pl.BlockSpec((None, lc, 2 * NA_HEAD_DIM), lambda p, bi, j: (bi, 0, p)),
                  pl.BlockSpec((None,) + tables.shape[1:], lambda p, bi, j: (p, 0, 0, 0, 0))],
        out_specs=pl.BlockSpec((None, tq, 2 * NA_HEAD_DIM), lambda p, bi, j: (bi, j, p)),
        out_shape=jax.ShapeDtypeStruct((b, l, aw), _BF16),
        scratch_shapes=[pltpu.VMEM((len(row_entries), 2 * GRID_W, NA_KH * GRID_W), _F32)],
        compiler_params=_cparams("arbitrary", "arbitrary", "arbitrary"),
        name="na_latent",
    )(q, k, v, kc, vc, tables)


def _ctx_attn_kernel(q_ref, k_ref, v_ref, o_ref):
    dh = NA_HEAD_DIM
    outs = []
    for h in range(2):
        ls = slice(h * dh, (h + 1) * dh)
        s = _dot_nt(q_ref[:, ls], k_ref[:, ls])
        m = jnp.max(s, axis=-1, keepdims=True)
        p = jnp.exp(s - m)
        den = jnp.sum(p, axis=-1, keepdims=True)
        outs.append(_dot(p.astype(_BF16), v_ref[:, ls]) / den)
    o_ref[...] = jnp.concatenate(outs, axis=-1).astype(_BF16)


def _ctx_attention(q, k, v):
    b, lc, aw = q.shape
    hp = aw // (2 * NA_HEAD_DIM)
    spec = pl.BlockSpec((None, lc, 2 * NA_HEAD_DIM), lambda bi, p: (bi, 0, p))
    return pl.pallas_call(
        _ctx_attn_kernel,
        grid=(b, hp),
        in_specs=[spec] * 3, out_specs=spec,
        out_shape=jax.ShapeDtypeStruct((b, lc, aw), _BF16),
        compiler_params=_cparams("parallel", "parallel"),
        name="ctx_attention",
    )(q, k, v)


def _sigmoid(z):
    return 0.5 * jnp.tanh(0.5 * z) + 0.5


def _gelu_tanh(x):
    return 0.5 * x * (1.0 + jnp.tanh(math.sqrt(2.0 / math.pi) * (x + 0.044715 * (x * x * x))))


def _merge_kernel(x_ref, mod_ref, h_ref, fa_ref, u_ref, yf_ref, yr_ref, nc_ref, d_ref, wglu_ref,
                  wga_ref, wgb_ref, wgc_ref, wa_ref, wb_ref, wc_ref, wo_ref, o_ref):
    x = x_ref[...]
    h = h_ref[...]
    y = u_ref[...].astype(_F32) * d_ref[...] + yf_ref[...].astype(_F32) + yr_ref[...].astype(_F32)
    gl = _gelu_tanh(y)
    sb = gl * _sigmoid(_dot(gl.astype(_BF16), wglu_ref[...]))
    m = _sigmoid(_dot(h, wga_ref[...])) * _dot(fa_ref[...], wa_ref[...])
    m += _sigmoid(_dot(h, wgb_ref[...])) * _dot(sb.astype(_BF16), wb_ref[...])
    m += _sigmoid(_dot(h, wgc_ref[...])) * _dot(nc_ref[...], wc_ref[...])
    o_ref[...] = x + mod_ref[2:3, :] * _dot(m.astype(_BF16), wo_ref[...])


def _merge(x, mod, h, fa, u, yf, yr, nc, d_skip, w_glu, w_in, w_a, w_b, w_c, w_out, layer, mod_row, off_g, tm):
    b, l, d = x.shape
    assert off_g % d == 0
    mod_spec, mat = _layer_specs(layer, mod_row, d)
    gate = lambda k: pl.BlockSpec((None, d, d), lambda bi, i: (layer, 0, off_g // d + k))
    tok = lambda a: pl.BlockSpec((None, tm, a.shape[2]), lambda bi, i: (bi, i, 0))
    tmaj = lambda a: pl.BlockSpec((tm, a.shape[1] // b), lambda bi, i: (i, bi))
    return pl.pallas_call(
        _merge_kernel,
        grid=(b, l // tm),
        in_specs=[tok(x), mod_spec, tok(h), tok(fa), tmaj(u), tmaj(yf), tmaj(yr), tok(nc), mat(d_skip), mat(w_glu),
                  gate(0), gate(1), gate(2), mat(w_a), mat(w_b), mat(w_c), mat(w_out)],
        out_specs=tok(x),
        out_shape=jax.ShapeDtypeStruct((b, l, d), _F32),
        compiler_params=_cparams("parallel", "parallel"),
        name="merge",
    )(x, mod, h, fa, u, yf, yr, nc, d_skip, w_glu, w_in, w_in, w_in, w_a, w_b, w_c, w_out)


def _ffn_kernel(x_ref, mod_ref, g_ref, w1_ref, w2_ref, gf_ref, o_ref, *, final_norm, fchunk):
    x = x_ref[...]
    h = _norm_mod(x, g_ref[...], mod_ref[3:4, :], mod_ref[4:5, :]).astype(_BF16)
    dff = w1_ref.shape[1]
    acc = jnp.zeros_like(x)
    for c in range(dff // fchunk):
        cs = slice(c * fchunk, (c + 1) * fchunk)
        t = jnp.maximum(_dot(h, w1_ref[:, cs]), 0.0)
        acc += _dot((t * t).astype(_BF16), w2_ref[cs, :])
    x2 = x + mod_ref[5:6, :] * acc
    if final_norm:
        ms = jnp.mean(x2 * x2, axis=-1, keepdims=True)
        x2 = x2 * lax.rsqrt(ms + RMS_EPS) * gf_ref[...]
    o_ref[...] = x2


def _ffn(x, mod, g, w1, w2, g_final, final_norm, layer, mod_row, tm):
    b, l, d = x.shape
    mod_spec, mat = _layer_specs(layer, mod_row, d)
    tok = pl.BlockSpec((None, tm, d), lambda bi, i: (bi, i, 0))
    full = lambda a: pl.BlockSpec(a.shape, lambda bi, i: (0, 0))
    return pl.pallas_call(
        functools.partial(_ffn_kernel, final_norm=final_norm, fchunk=FFN_CHUNK),
        grid=(b, l // tm),
        in_specs=[tok, mod_spec, mat(g), mat(w1), mat(w2), full(g_final)],
        out_specs=tok,
        out_shape=jax.ShapeDtypeStruct((b, l, d), _F32),
        compiler_params=_cparams("parallel", "parallel"),
        name="ffn",
    )(x, mod, g, w1, w2, g_final)


def kernel(x, c, ctx, c_ctx, w_mod, b_mod, g_norm1, g_norm2, w_in, w_br_a, w_br_b, w_br_c, w_out, s5_lam_re, s5_lam_im, s5_log_dt, s5_b_re, s5_b_im, s5_c_re, s5_c_im, s5_d, s5_w_glu, na_rpb, w_ff1, w_ff2, g_final):
    b, l, d = x.shape
    lc = ctx.shape[1]
    depth = w_mod.shape[0]
    fw = w_br_a.shape[1]
    sw = w_br_b.shape[1]
    aw = w_br_c.shape[1]
    off_g = fw + sw + 3 * aw
    tm = TOKEN_TILE
    tmc = lc

    c8 = jnp.concatenate([c, c_ctx[None, :], jnp.zeros((SUBLANES - b - 1, d), _F32)], axis=0)
    mod = _modulation(c8, w_mod, b_mod).reshape(depth, SUBLANES, 6, d)
    gf = g_final.reshape(1, d)
    g1 = g_norm1.reshape(depth, 1, d)
    g2 = g_norm2.reshape(depth, 1, d)
    d_skip = s5_d.reshape(depth, 1, sw)
    bf = lambda t: t.astype(_BF16)
    w_in_b, w_a, w_b, w_c, w_o, w_glu, w1, w2 = (bf(w_in), bf(w_br_a), bf(w_br_b), bf(w_br_c), bf(w_out),
                                                 bf(s5_w_glu), bf(w_ff1), bf(w_ff2))
    widths = (fw, sw, aw)

    for i in range(depth):
        need_ctx_out = i < depth - 1
        h_x, zf_x, zs_x, q_x, k_x, v_x = _inproj(x, mod, g1, w_in_b, i, None, widths, tm)
        h_c, zf_c, zs_c, q_c, k_c, v_c = _inproj(ctx, mod, g1, w_in_b, i, b, widths, tmc)

        fa_x = _fnet_latent(zf_x)
        (yf_x, yr_x), (yf_c, yr_c) = _s5_mix(zs_x, zs_c, b, s5_lam_re[i], s5_lam_im[i], s5_log_dt[i],
                                              s5_b_re[i], s5_b_im[i], s5_c_re[i], s5_c_im[i])
        nc_x = _na_latent(q_x, k_x, v_x, k_c, v_c, na_rpb[i])

        branch_w = (d_skip, w_glu, w_in_b, w_a, w_b, w_c, w_o)
        x = _merge(x, mod, h_x, fa_x, zs_x, yf_x, yr_x, nc_x, *branch_w, i, None, off_g, tm)
        x = _ffn(x, mod, g2, w1, w2, gf, not need_ctx_out, i, None, tm)

        if need_ctx_out:
            fa_c = _fnet_ctx(zf_c)
            nc_c = _ctx_attention(q_c, k_c, v_c)
            ctx = _merge(ctx, mod, h_c, fa_c, zs_c, yf_c, yr_c, nc_c, *branch_w, i, b, off_g, tmc)
            ctx = _ffn(ctx, mod, g2, w1, w2, gf, False, i, b, tmc)
    return x
```

```python
import functools
import math

import numpy as np
import jax
import jax.numpy as jnp
from jax import lax
from jax.experimental import pallas as pl
from jax.experimental.pallas import tpu as pltpu

_F32 = jnp.float32
_BF16 = jnp.bfloat16

RMS_EPS = 1e-6
GRID_W = 64
F_GROUP_DIM = 64
NA_HEAD_DIM = 64
NA_KH = 8
NA_KW = 16
NA_QROWS = 32
NEG_BIG = -1e30
FFT_L2 = 128
FFT_T2_STEP = 32
FFT_K1_STEP = 16
S5_TS = 64
TOKEN_TILE = 1024
FFN_CHUNK = 1024
MOD_TN = 1024

SUBLANES = 8
VMEM_LIMIT_BYTES = 56 * 1024 * 1024


def _cparams(*sem):
    return pltpu.CompilerParams(dimension_semantics=sem, vmem_limit_bytes=VMEM_LIMIT_BYTES)


def _dot(a, b):
    return jnp.dot(a, b, preferred_element_type=_F32)


def _table_bf16(t):
    return jnp.asarray(t, _F32).astype(_BF16)


def _dot_nt(a, b):
    return lax.dot_general(a, b, (((1,), (1,)), ((), ())), preferred_element_type=_F32)


def _norm_mod(x, g, shift, scale):
    ms = jnp.mean(x * x, axis=-1, keepdims=True)
    y = x * lax.rsqrt(ms + RMS_EPS) * g
    return y * (1.0 + scale) + shift


def _mod_kernel(c_ref, w_ref, b_ref, o_ref):
    c = c_ref[...]
    s = c * jax.nn.sigmoid(c)
    o_ref[...] = jnp.dot(s, w_ref[...], precision=lax.Precision.HIGHEST,
                         preferred_element_type=_F32) + b_ref[...]


def _modulation(c8, w_mod, b_mod):
    depth, d, n = w_mod.shape
    tn = MOD_TN
    rows = c8.shape[0]
    return pl.pallas_call(
        _mod_kernel,
        grid=(depth, n // tn),
        in_specs=[pl.BlockSpec((rows, d), lambda l, j: (0, 0)),
                  pl.BlockSpec((None, d, tn), lambda l, j: (l, 0, j)),
                  pl.BlockSpec((None, 1, tn), lambda l, j: (l, 0, j))],
        out_specs=pl.BlockSpec((None, rows, tn), lambda l, j: (l, 0, j)),
        out_shape=jax.ShapeDtypeStruct((depth, rows, n), _F32),
        compiler_params=_cparams("parallel", "parallel"),
        name="modulation",
    )(c8, w_mod, b_mod.reshape(depth, 1, n))


def _inproj_kernel(x_ref, mod_ref, g_ref, w_ref, h_ref, zf_ref, zs_ref, q_ref, k_ref, v_ref, *, widths, q_scale):
    h = _norm_mod(x_ref[...], g_ref[...], mod_ref[0:1, :], mod_ref[1:2, :]).astype(_BF16)
    h_ref[...] = h
    fw, sw, aw = widths
    o = 0
    zf_ref[...] = _dot(h, w_ref[:, o:o + fw]).astype(_BF16)
    o += fw
    zs_ref[...] = _dot(h, w_ref[:, o:o + sw]).astype(_BF16)
    o += sw
    q_ref[...] = (_dot(h, w_ref[:, o:o + aw]) * q_scale).astype(_BF16)
    o += aw
    k_ref[...] = _dot(h, w_ref[:, o:o + aw]).astype(_BF16)
    o += aw
    v_ref[...] = _dot(h, w_ref[:, o:o + aw]).astype(_BF16)


def _layer_specs(layer, mod_row, d):
    mod = pl.BlockSpec((None, None, 6, d), lambda bi, i: (layer, bi if mod_row is None else mod_row, 0, 0))
    mat = lambda a: pl.BlockSpec((None,) + a.shape[1:], lambda bi, i: (layer, 0, 0))
    return mod, mat


def _inproj(x, mod, g, w, layer, mod_row, widths, tm):
    b, l, d = x.shape
    fw, sw, aw = widths
    n = fw + sw + 3 * aw
    mod_spec, mat = _layer_specs(layer, mod_row, d)
    tok = lambda width: pl.BlockSpec((None, tm, width), lambda bi, i: (bi, i, 0))
    return pl.pallas_call(
        functools.partial(_inproj_kernel, widths=widths, q_scale=NA_HEAD_DIM ** -0.5),
        grid=(b, l // tm),
        in_specs=[tok(d), mod_spec, mat(g), pl.BlockSpec((None, d, n), lambda bi, i: (layer, 0, 0))],
        out_specs=[tok(d), tok(fw), pl.BlockSpec((tm, sw), lambda bi, i: (i, bi)), tok(aw), tok(aw), tok(aw)],
        out_shape=[jax.ShapeDtypeStruct((b, l, d), _BF16), jax.ShapeDtypeStruct((b, l, fw), _BF16), jax.ShapeDtypeStruct((l, b * sw), _BF16),
                   jax.ShapeDtypeStruct((b, l, aw), _BF16), jax.ShapeDtypeStruct((b, l, aw), _BF16),
                   jax.ShapeDtypeStruct((b, l, aw), _BF16)],
        compiler_params=_cparams("parallel", "parallel"),
        name="inproj",
    )(x, mod, g, w)


def _dft_tables(l):
    l1, l2 = l // FFT_L2, FFT_L2
    a1 = 2 * np.pi * np.outer(np.arange(l1), np.arange(l1)) / l1
    f1 = np.concatenate([np.cos(a1), np.sin(a1)], axis=0)
    at = 2 * np.pi * np.outer(np.arange(l1), np.arange(l2)) / l
    a2 = 2 * np.pi * np.outer(np.arange(l2), np.arange(l2)) / l2
    c2, s2 = np.cos(a2), np.sin(a2)
    f2 = np.block([[c2, -s2], [s2, c2]])
    return f1, np.cos(at), np.sin(at), f2


def _group_dft_tables(width):
    ac = 2 * np.pi * np.outer(np.arange(F_GROUP_DIM), np.arange(F_GROUP_DIM)) / F_GROUP_DIM
    eye = np.eye(width // F_GROUP_DIM)
    return np.kron(eye, np.cos(ac)), np.kron(eye, np.sin(ac))


def _fnet_a_kernel(x_ref, f1_ref, cw_ref, sw_ref, o_ref):
    a = _dot(f1_ref[...], x_ref[...])
    l1 = a.shape[0] // 2
    ac, asn = a[:l1], a[l1:]
    cw, sw = cw_ref[...], sw_ref[...]
    o_ref[0] = (ac * cw - asn * sw).astype(_BF16)
    o_ref[1] = (ac * sw + asn * cw).astype(_BF16)


def _fnet_b_kernel(a_ref, f2_ref, bc_ref, bs_ref, o_ref, *, scale, nk1, width):
    half = f2_ref.shape[0] // 2
    for j in range(nk1):
        rhs = jnp.concatenate([a_ref[0, j], a_ref[1, j]], axis=0)
        r = _dot(f2_ref[...], rhs)
        out = _dot(r[:half].astype(_BF16), bc_ref[...]) - _dot(r[half:].astype(_BF16), bs_ref[...])
        o_ref[:, j * width:(j + 1) * width] = (out * scale).astype(_BF16)


def _fnet_latent(zf):
    b, l, width = zf.shape
    l1, l2 = l // FFT_L2, FFT_L2
    f1, cw, sw, f2 = _dft_tables(l)
    bc, bs = _group_dft_tables(width)
    nc = FFT_T2_STEP * width
    cw = jnp.repeat(jnp.asarray(cw, _F32), width, axis=1)
    sw = jnp.repeat(jnp.asarray(sw, _F32), width, axis=1)
    a = pl.pallas_call(
        _fnet_a_kernel,
        grid=(l2 * width // nc, b),
        in_specs=[pl.BlockSpec((None, l1, nc), lambda j, bi: (bi, 0, j)),
                  pl.BlockSpec((2 * l1, l1), lambda j, bi: (0, 0)),
                  pl.BlockSpec((l1, nc), lambda j, bi: (0, j)),
                  pl.BlockSpec((l1, nc), lambda j, bi: (0, j))],
        out_specs=pl.BlockSpec((None, 2, l1, nc), lambda j, bi: (bi, 0, 0, j)),
        out_shape=jax.ShapeDtypeStruct((b, 2, l1, l2 * width), _BF16),
        compiler_params=_cparams("parallel", "parallel"),
        name="fnet_a",
    )(zf.reshape(b, l1, l2 * width), _table_bf16(f1), cw, sw)
    nk1 = FFT_K1_STEP
    out = pl.pallas_call(
        functools.partial(_fnet_b_kernel, scale=float((l * F_GROUP_DIM) ** -0.5), nk1=nk1, width=width),
        grid=(b, l1 // nk1),
        in_specs=[pl.BlockSpec((None, 2, nk1, l2, width), lambda bi, j: (bi, 0, j, 0, 0)),
                  pl.BlockSpec((2 * l2, 2 * l2), lambda bi, j: (0, 0)),
                  pl.BlockSpec((width, width), lambda bi, j: (0, 0)),
                  pl.BlockSpec((width, width), lambda bi, j: (0, 0))],
        out_specs=pl.BlockSpec((None, l2, nk1 * width), lambda bi, j: (bi, 0, j)),
        out_shape=jax.ShapeDtypeStruct((b, l2, l1 * width), _BF16),
        compiler_params=_cparams("parallel", "parallel"),
        name="fnet_b",
    )(a.reshape(b, 2, l1, l2, width), _table_bf16(f2), _table_bf16(bc), _table_bf16(bs))
    return out.reshape(b, l, width)


def _fnet_ctx_kernel(z_ref, cl_ref, sl_ref, bc_ref, bs_ref, o_ref, *, scale):
    z = z_ref[...]
    zc = _dot(z, bc_ref[...]).astype(_BF16)
    zs = _dot(z, bs_ref[...]).astype(_BF16)
    o_ref[...] = ((_dot(cl_ref[...], zc) - _dot(sl_ref[...], zs)) * scale).astype(_BF16)


def _fnet_ctx(zf):
    b, l, width = zf.shape
    al = 2 * np.pi * np.outer(np.arange(l), np.arange(l)) / l
    bc, bs = _group_dft_tables(width)
    full = lambda r, c: pl.BlockSpec((r, c), lambda bi: (0, 0))
    return pl.pallas_call(
        functools.partial(_fnet_ctx_kernel, scale=float((l * F_GROUP_DIM) ** -0.5)),
        grid=(b,),
        in_specs=[pl.BlockSpec((None, l, width), lambda bi: (bi, 0, 0)),
                  full(l, l), full(l, l), full(width, width), full(width, width)],
        out_specs=pl.BlockSpec((None, l, width), lambda bi: (bi, 0, 0)),
        out_shape=jax.ShapeDtypeStruct((b, l, width), _BF16),
        compiler_params=_cparams("parallel"),
        name="fnet_ctx",
    )(zf, _table_bf16(np.cos(al)), _table_bf16(np.sin(al)), _table_bf16(bc), _table_bf16(bs))


def _s5_disc_kernel(lr_ref, li_ref, ldt_ref, br_ref, bi_ref, lbr_ref, lbi_ref, bbr_ref, bbi_ref):
    lr = jnp.minimum(lr_ref[...], -1e-4)
    li = li_ref[...]
    dt = jnp.exp(ldt_ref[...])
    mag = jnp.exp(lr * dt)
    lbr = mag * jnp.cos(li * dt)
    lbi = mag * jnp.sin(li * dt)
    lbr_ref[...] = lbr
    lbi_ref[...] = lbi
    a, bb = lbr - 1.0, lbi
    den = lr * lr + li * li
    cr = (a * lr + bb * li) / den
    ci = (bb * lr - a * li) / den
    br, bi = br_ref[...], bi_ref[...]
    bbr_ref[...] = cr * br - ci * bi
    bbi_ref[...] = cr * bi + ci * br


def _s5_discretise(lam_re, lam_im, log_dt, b_re, b_im):
    _, g, p, n = b_re.shape
    rows = 2 * g * n
    ex = lambda t: jnp.broadcast_to(t[:, :, None, :], (2, g, n, p)).reshape(rows, p)
    ldt = jnp.broadcast_to(log_dt[:, :, None, None], (2, g, n, p)).reshape(rows, p)
    tb = lambda t: t.transpose(0, 1, 3, 2).reshape(rows, p)
    spec = pl.BlockSpec((rows, p), lambda: (0, 0))
    sds = jax.ShapeDtypeStruct((rows, p), _F32)
    lbr, lbi, bbr, bbi = pl.pallas_call(
        _s5_disc_kernel,
        in_specs=[spec] * 5, out_specs=[spec] * 4, out_shape=[sds] * 4,
        name="s5_discretise",
    )(ex(lam_re), ex(lam_im), ldt, tb(b_re), tb(b_im))
    shp = (2, g, n, p)
    return lbr.reshape(shp)[:, :, 0, :], lbi.reshape(shp)[:, :, 0, :], bbr.reshape(shp), bbi.reshape(shp)


def _s5_scan_kernel(ucf_ref, uxf_ref, ucr_ref, uxr_ref, perm_ref, permt_ref, bf_ref, br_ref, l_re_ref, l_im_ref,
                    cf_re_ref, cf_im_ref, cr_re_ref, cr_im_ref,
                    yfc_ref, yfx_ref, yrc_ref, yrx_ref,
                    d0a, d0b, d0c, d0d, d1a, d1b, d1c, d1d, h0a, h0b, h0c, h0d, h1a, h1b, h1c, h1d,
                    st_re, st_im, *, ts, nb, nc2, nst, sw):
    g = pl.program_id(0)
    d0, d1 = (d0a, d0b, d0c, d0d), (d1a, d1b, d1c, d1d)
    h0, h1 = (h0a, h0b, h0c, h0d), (h1a, h1b, h1c, h1d)

    @pl.when(g == 0)
    def _():
        for r in d0 + d1 + h0 + h1 + (st_re, st_im):
            r[...] = jnp.zeros_like(r)

    rt = SUBLANES
    ntile = ts // 2
    lower = lax.broadcasted_iota(jnp.int32, (rt, nst), 0) < nb
    is_ctx_in = g < nc2
    u_f = jnp.where(is_ctx_in, ucf_ref[...], uxf_ref[...])
    u_r = jnp.where(is_ctx_in, ucr_ref[...], uxr_ref[...])

    def time_batch_rows(u):
        stack = jnp.concatenate([u[:, b * sw:(b + 1) * sw] for b in range(nb)], axis=0)
        return _dot(perm_ref[...], stack).astype(_BF16)

    def drive(uf, ur, d):
        uf, ur = time_batch_rows(uf), time_batch_rows(ur)
        d[0][...] = _dot(uf, bf_ref[:, 0:nst])
        d[1][...] = _dot(uf, bf_ref[:, nst:2 * nst])
        d[2][...] = _dot(ur, br_ref[:, 0:nst])
        d[3][...] = _dot(ur, br_ref[:, nst:2 * nst])

    def scan(d, h):
        l_re, l_im = l_re_ref[...], l_im_ref[...]
        x_re, x_im = st_re[...], st_im[...]

        def step(x_re, x_im, s_re, s_im):
            return l_re * x_re - l_im * x_im + s_re, l_re * x_im + l_im * x_re + s_im

        for k in range(ntile // 2):
            f_tiles, r_tiles = [], []
            for m in (2 * k, 2 * k + 1):
                tf = slice(rt * m, rt * (m + 1))
                tr = slice(rt * (ntile - 1 - m), rt * (ntile - m))
                a_re, a_im, b_re, b_im = d[0][tf, :], d[1][tf, :], d[2][tr, :], d[3][tr, :]
                e_re, e_im = step(x_re, x_im, jnp.where(lower, a_re, b_re), jnp.where(lower, a_im, b_im))
                x_re, x_im = step(e_re, e_im, pltpu.roll(jnp.where(lower, b_re, a_re), nb, 0),
                                  pltpu.roll(jnp.where(lower, b_im, a_im), nb, 0))
                o_re, o_im = pltpu.roll(x_re, nb, 0), pltpu.roll(x_im, nb, 0)
                f_tiles.append((jnp.where(lower, e_re, o_re), jnp.where(lower, e_im, o_im)))
                r_tiles.append((jnp.where(lower, o_re, e_re), jnp.where(lower, o_im, e_im)))
            pf = slice(2 * rt * k, 2 * rt * (k + 1))
            pr = slice(2 * rt * (ntile // 2 - 1 - k), 2 * rt * (ntile // 2 - k))
            pack = lambda lo_t, hi_t: jnp.concatenate([lo_t, hi_t], axis=0).astype(_BF16)
            h[0][pf, :] = pack(f_tiles[0][0], f_tiles[1][0])
            h[1][pf, :] = pack(f_tiles[0][1], f_tiles[1][1])
            h[2][pr, :] = pack(r_tiles[1][0], r_tiles[0][0])
            h[3][pr, :] = pack(r_tiles[1][1], r_tiles[0][1])
        st_re[...] = x_re
        st_im[...] = x_im

    def readout(h):
        yf = (_dot(h[0][...], cf_re_ref[...]) - _dot(h[1][...], cf_im_ref[...])).astype(_BF16)
        yr = (_dot(h[2][...], cr_re_ref[...]) - _dot(h[3][...], cr_im_ref[...])).astype(_BF16)
        return _dot(permt_ref[...], yf).astype(_BF16), _dot(permt_ref[...], yr).astype(_BF16)

    lo, hi = slice(0, ts), slice(ts, 2 * ts)
    scan(d1, h1)
    drive(u_f[lo], u_r[hi], d0)
    yf0, yr0 = readout(h0)
    drive(u_f[hi], u_r[lo], d1)
    scan(d0, h0)
    yf1, yr1 = readout(h1)

    def store(yf_ref, yr_ref):
        for b in range(nb):
            cs = slice(b * sw, (b + 1) * sw)
            rows = slice(b * ts, (b + 1) * ts)
            yf_ref[lo, cs] = yf0[rows]
            yf_ref[hi, cs] = yf1[rows]
            yr_ref[hi, cs] = yr0[rows]
            yr_ref[lo, cs] = yr1[rows]

    pl.when(jnp.logical_and(g >= 1, g <= nc2))(lambda: store(yfc_ref, yrc_ref))
    pl.when(g > nc2)(lambda: store(yfx_ref, yrx_ref))


def _s5_scan(u_c, u_x, bf, br, l_re, l_im, cf_re, cf_im, cr_re, cr_im, nb, ts):
    sw = u_x.shape[1] // nb
    nst = l_re.shape[1]
    blk = 2 * ts
    nc2, nx2 = u_c.shape[0] // blk, u_x.shape[0] // blk
    nt2 = nc2 + nx2
    perm = np.zeros((ts * nb, ts * nb), np.float32)
    for t in range(ts):
        for b in range(nb):
            perm[t * nb + b, b * ts + t] = 1.0
    full = lambda a: pl.BlockSpec(a.shape, lambda g: (0, 0))
    spec = lambda f: pl.BlockSpec((blk, nb * sw), lambda g: (f(g), 0))
    clip = lambda v, n: jnp.clip(v, 0, n - 1)
    ins = [spec(lambda g: clip(g, nc2)), spec(lambda g: clip(g - nc2, nx2)),
           spec(lambda g: clip(nc2 - 1 - g, nc2)), spec(lambda g: clip(nt2 - 1 - g, nx2))]
    outs = [spec(lambda g: clip(g - 1, nc2)), spec(lambda g: clip(g - 1 - nc2, nx2)),
            spec(lambda g: clip(nc2 - g, nc2)), spec(lambda g: clip(nt2 - g, nx2))]
    perm_b, permt_b = _table_bf16(perm), _table_bf16(perm.T)
    sds = lambda a: jax.ShapeDtypeStruct(a.shape, _BF16)
    return pl.pallas_call(
        functools.partial(_s5_scan_kernel, ts=ts, nb=nb, nc2=nc2, nst=nst, sw=sw),
        grid=(nt2 + 1,),
        in_specs=ins + [full(perm_b), full(permt_b), full(bf), full(br), full(l_re), full(l_im),
                        full(cf_re), full(cf_im), full(cr_re), full(cr_im)],
        out_specs=outs,
        out_shape=[sds(u_c), sds(u_x), sds(u_c), sds(u_x)],
        scratch_shapes=([pltpu.VMEM((ts * nb, nst), _F32)] * 8 + [pltpu.VMEM((ts * nb, nst), _BF16)] * 8
                        + [pltpu.VMEM((SUBLANES, nst), _F32)] * 2),
        compiler_params=_cparams("arbitrary"),
        name="s5_scan",
    )(u_c, u_x, u_c, u_x, perm_b, permt_b, bf, br, l_re, l_im, cf_re, cf_im, cr_re, cr_im)


def _block_diag(t):
    g, r, c = t.shape
    eye = jnp.eye(g, dtype=t.dtype)
    return (t[:, :, None, :] * eye[:, None, :, None]).reshape(g * r, g * c)


def _s5_mix(zs_x, zs_c, nb, lam_re, lam_im, log_dt, b_re, b_im, c_re, c_im):
    assert 2 * nb == SUBLANES, "the scan state tile packs batch x direction on the sublanes"
    lbr, lbi, bbr, bbi = _s5_discretise(lam_re, lam_im, log_dt, b_re, b_im)
    nst = lbr.shape[1] * lbr.shape[2]
    drive = lambda d: jnp.concatenate([_block_diag(bbr[d]), _block_diag(bbi[d])], axis=1).astype(_BF16)
    rd = lambda t: _block_diag(t.transpose(0, 2, 1)).astype(_BF16)
    tile = lambda t: jnp.concatenate([jnp.broadcast_to(t[0].reshape(1, nst), (nb, nst)),
                                      jnp.broadcast_to(t[1].reshape(1, nst), (nb, nst))], axis=0)
    yf_c, yf_x, yr_c, yr_x = _s5_scan(zs_c, zs_x, drive(0), drive(1), tile(lbr), tile(lbi),
                                      rd(c_re[0]), rd(c_im[0]), rd(c_re[1]), rd(c_im[1]), nb, S5_TS)
    return (yf_x, yr_x), (yf_c, yr_c)


def _na_row_entries(rows):
    r = np.concatenate([np.arange(NA_QROWS), [NA_QROWS], rows - NA_QROWS + np.arange(NA_QROWS)])
    rs = np.clip(r - NA_KH // 2, 0, rows - NA_KH)
    idx = rs[:, None] + np.arange(NA_KH)[None, :] - r[:, None] + NA_KH - 1
    return tuple(tuple(int(v) for v in row) for row in idx)


def _na_table_kernel(rpb_ref, sel_ref, valid_ref, o_ref):
    t = jnp.dot(rpb_ref[...], sel_ref[...], precision=lax.Precision.HIGHEST, preferred_element_type=_F32)
    o_ref[...] = jnp.where(valid_ref[...] > 0.0, t, NEG_BIG)


def _na_column_tables(rpb):
    h, nr, ncol = rpb.shape
    w = np.arange(GRID_W)
    kc = np.arange(GRID_W)
    cs = np.clip(w - NA_KW // 2, 0, GRID_W - NA_KW)
    valid_c = (kc[None, :] >= cs[:, None]) & (kc[None, :] < cs[:, None] + NA_KW)
    idx_c = np.clip(kc[None, :] - w[:, None] + NA_KW - 1, 0, ncol - 1).reshape(-1)
    kpad = -(-ncol // SUBLANES) * SUBLANES
    sel = np.zeros((kpad, GRID_W * GRID_W), np.float32)
    sel[idx_c, np.arange(GRID_W * GRID_W)] = 1.0
    full = lambda r, c: pl.BlockSpec((r, c), lambda: (0, 0))
    out = pl.pallas_call(
        _na_table_kernel,
        in_specs=[full(h * nr, kpad), full(kpad, GRID_W * GRID_W), full(1, GRID_W * GRID_W)],
        out_specs=full(h * nr, GRID_W * GRID_W),
        out_shape=jax.ShapeDtypeStruct((h * nr, GRID_W * GRID_W), _F32),
        name="na_tables",
    )(jnp.pad(rpb.reshape(h * nr, ncol), ((0, 0), (0, kpad - ncol))), jnp.asarray(sel),
      jnp.asarray(valid_c.reshape(1, -1).astype(np.float32)))
    return out.reshape(h, nr, GRID_W, GRID_W)


def _na_kernel(q_ref, k_ref, v_ref, kc_ref, vc_ref, t_ref, o_ref, tab_ref, *, rows, nb, row_entries):
    blk = pl.program_id(2)
    dh = NA_HEAD_DIM
    win = NA_KH * GRID_W
    gw = GRID_W

    @pl.when(jnp.logical_and(pl.program_id(1) == 0, blk == 0))
    def _():
        for e, idx in enumerate(row_entries):
            for h in range(2):
                tab_ref[e, h * gw:(h + 1) * gw, :] = jnp.concatenate([t_ref[h, a] for a in idx], axis=1)

    q = q_ref[...]
    head0 = lax.broadcasted_iota(jnp.int32, q.shape, 1) < dh
    zero = jnp.zeros_like(q)
    q0, q1 = jnp.where(head0, q, zero), jnp.where(head0, zero, q)
    qq = jnp.concatenate([t[dr * gw:(dr + 1) * gw] for dr in range(NA_QROWS) for t in (q0, q1)], axis=0)
    starts, entries = [], []
    for dr in range(NA_QROWS):
        r = blk * NA_QROWS + dr
        rs = jnp.clip(r - NA_KH // 2, 0, rows - NA_KH)
        starts.append(pl.multiple_of(rs * gw, gw))
        entries.append(jnp.where(blk == 0, dr, jnp.where(blk == nb - 1, NA_QROWS + 1 + dr, NA_QROWS)))
    pair = lambda dr: slice(2 * dr * gw, 2 * (dr + 1) * gw)
    s_ctx = _dot_nt(qq, kc_ref[...])
    s_loc = jnp.concatenate(
        [_dot_nt(qq[pair(dr)], k_ref[pl.ds(starts[dr], win), :]) + tab_ref[entries[dr]]
         for dr in range(NA_QROWS)], axis=0)
    m = jnp.maximum(jnp.max(s_loc, axis=-1, keepdims=True), jnp.max(s_ctx, axis=-1, keepdims=True))
    p_loc = jnp.exp(s_loc - m)
    p_ctx = jnp.exp(s_ctx - m)
    den = jnp.sum(p_loc, axis=-1, keepdims=True) + jnp.sum(p_ctx, axis=-1, keepdims=True)
    p_loc = p_loc.astype(_BF16)
    o = jnp.concatenate(
        [_dot(p_loc[pair(dr)], v_ref[pl.ds(starts[dr], win), :]) for dr in range(NA_QROWS)], axis=0)
    o = (o + _dot(p_ctx.astype(_BF16), vc_ref[...])) / den
    h0_lanes = lax.broadcasted_iota(jnp.int32, (gw, q.shape[1]), 1) < dh
    for dr in range(NA_QROWS):
        o_ref[dr * gw:(dr + 1) * gw, :] = jnp.where(
            h0_lanes, o[2 * dr * gw:(2 * dr + 1) * gw], o[(2 * dr + 1) * gw:(2 * dr + 2) * gw]).astype(_BF16)


def _na_latent(q, k, v, kc, vc, rpb):
    b, l, aw = q.shape
    lc = kc.shape[1]
    rows = l // GRID_W
    assert rows % NA_QROWS == 0 and rows >= 2 * NA_QROWS
    nb = rows // NA_QROWS
    hp = aw // (2 * NA_HEAD_DIM)
    tq = NA_QROWS * GRID_W
    row_entries = _na_row_entries(rows)
    tables = _na_column_tables(rpb.astype(_F32))
    tables = tables.reshape((hp, 2) + tables.shape[1:])
    return pl.pallas_call(
        functools.partial(_na_kernel, rows=rows, nb=nb, row_entries=row_entries),
        grid=(hp, b, nb),
        in_specs=[pl.BlockSpec((None, tq, 2 * NA_HEAD_DIM), lambda p, bi, j: (bi, j, p)),
                  pl.BlockSpec((None, l, 2 * NA_HEAD_DIM), lambda p, bi, j: (bi, 0, p)),
                  pl.BlockSpec((None, l, 2 * NA_HEAD_DIM), lambda p, bi, j: (bi, 0, p)),
                  pl.BlockSpec((None, lc, 2 * NA_HEAD_DIM), lambda p, bi, j: (bi, 0, p)),
                  pl.BlockSpec((None, lc, 2 * NA_HEAD_DIM), lambda p, bi, j: (bi, 0, p)),
                  pl.BlockSpec((None,) + tables.shape[1:], lambda p, bi, j: (p, 0, 0, 0, 0))],
        out_specs=pl.BlockSpec((None, tq, 2 * NA_HEAD_DIM), lambda p, bi, j: (bi, j, p)),
        out_shape=jax.ShapeDtypeStruct((b, l, aw), _BF16),
        scratch_shapes=[pltpu.VMEM((len(row_entries), 2 * GRID_W, NA_KH * GRID_W), _F32)],
        compiler_params=_cparams("arbitrary", "arbitrary", "arbitrary"),
        name="na_latent",
    )(q, k, v, kc, vc, tables)


def _ctx_attn_kernel(q_ref, k_ref, v_ref, o_ref):
    dh = NA_HEAD_DIM
    outs = []
    for h in range(2):
        ls = slice(h * dh, (h + 1) * dh)
        s = _dot_nt(q_ref[:, ls], k_ref[:, ls])
        m = jnp.max(s, axis=-1, keepdims=True)
        p = jnp.exp(s - m)
        den = jnp.sum(p, axis=-1, keepdims=True)
        outs.append(_dot(p.astype(_BF16), v_ref[:, ls]) / den)
    o_ref[...] = jnp.concatenate(outs, axis=-1).astype(_BF16)


def _ctx_attention(q, k, v):
    b, lc, aw = q.shape
    hp = aw // (2 * NA_HEAD_DIM)
    spec = pl.BlockSpec((None, lc, 2 * NA_HEAD_DIM), lambda bi, p: (bi, 0, p))
    return pl.pallas_call(
        _ctx_attn_kernel,
        grid=(b, hp),
        in_specs=[spec] * 3, out_specs=spec,
        out_shape=jax.ShapeDtypeStruct((b, lc, aw), _BF16),
        compiler_params=_cparams("parallel", "parallel"),
        name="ctx_attention",
    )(q, k, v)


def _sigmoid(z):
    return 0.5 * jnp.tanh(0.5 * z) + 0.5


def _gelu_tanh(x):
    return 0.5 * x * (1.0 + jnp.tanh(math.sqrt(2.0 / math.pi) * (x + 0.044715 * (x * x * x))))


def _merge_kernel(x_ref, mod_ref, h_ref, fa_ref, u_ref, yf_ref, yr_ref, nc_ref, d_ref, wglu_ref,
                  wga_ref, wgb_ref, wgc_ref, wa_ref, wb_ref, wc_ref, wo_ref, o_ref):
    x = x_ref[...]
    h = h_ref[...]
    y = u_ref[...].astype(_F32) * d_ref[...] + yf_ref[...].astype(_F32) + yr_ref[...].astype(_F32)
    gl = _gelu_tanh(y)
    sb = gl * _sigmoid(_dot(gl.astype(_BF16), wglu_ref[...]))
    m = _sigmoid(_dot(h, wga_ref[...])) * _dot(fa_ref[...], wa_ref[...])
    m += _sigmoid(_dot(h, wgb_ref[...])) * _dot(sb.astype(_BF16), wb_ref[...])
    m += _sigmoid(_dot(h, wgc_ref[...])) * _dot(nc_ref[...], wc_ref[...])
    o_ref[...] = x + mod_ref[2:3, :] * _dot(m.astype(_BF16), wo_ref[...])


def _merge(x, mod, h, fa, u, yf, yr, nc, d_skip, w_glu, w_in, w_a, w_b, w_c, w_out, layer, mod_row, off_g, tm):
    b, l, d = x.shape
    assert off_g % d == 0
    mod_spec, mat = _layer_specs(layer, mod_row, d)
    gate = lambda k: pl.BlockSpec((None, d, d), lambda bi, i: (layer, 0, off_g // d + k))
    tok = lambda a: pl.BlockSpec((None, tm, a.shape[2]), lambda bi, i: (bi, i, 0))
    tmaj = lambda a: pl.BlockSpec((tm, a.shape[1] // b), lambda bi, i: (i, bi))
    return pl.pallas_call(
        _merge_kernel,
        grid=(b, l // tm),
        in_specs=[tok(x), mod_spec, tok(h), tok(fa), tmaj(u), tmaj(yf), tmaj(yr), tok(nc), mat(d_skip), mat(w_glu),
                  gate(0), gate(1), gate(2), mat(w_a), mat(w_b), mat(w_c), mat(w_out)],
        out_specs=tok(x),
        out_shape=jax.ShapeDtypeStruct((b, l, d), _F32),
        compiler_params=_cparams("parallel", "parallel"),
        name="merge",
    )(x, mod, h, fa, u, yf, yr, nc, d_skip, w_glu, w_in, w_in, w_in, w_a, w_b, w_c, w_out)


def _ffn_kernel(x_ref, mod_ref, g_ref, w1_ref, w2_ref, gf_ref, o_ref, *, final_norm, fchunk):
    x = x_ref[...]
    h = _norm_mod(x, g_ref[...], mod_ref[3:4, :], mod_ref[4:5, :]).astype(_BF16)
    dff = w1_ref.shape[1]
    acc = jnp.zeros_like(x)
    for c in range(dff // fchunk):
        cs = slice(c * fchunk, (c + 1) * fchunk)
        t = jnp.maximum(_dot(h, w1_ref[:, cs]), 0.0)
        acc += _dot((t * t).astype(_BF16), w2_ref[cs, :])
    x2 = x + mod_ref[5:6, :] * acc
    if final_norm:
        ms = jnp.mean(x2 * x2, axis=-1, keepdims=True)
        x2 = x2 * lax.rsqrt(ms + RMS_EPS) * gf_ref[...]
    o_ref[...] = x2


def _ffn(x, mod, g, w1, w2, g_final, final_norm, layer, mod_row, tm):
    b, l, d = x.shape
    mod_spec, mat = _layer_specs(layer, mod_row, d)
    tok = pl.BlockSpec((None, tm, d), lambda bi, i: (bi, i, 0))
    full = lambda a: pl.BlockSpec(a.shape, lambda bi, i: (0, 0))
    return pl.pallas_call(
        functools.partial(_ffn_kernel, final_norm=final_norm, fchunk=FFN_CHUNK),
        grid=(b, l // tm),
        in_specs=[tok, mod_spec, mat(g), mat(w1), mat(w2), full(g_final)],
        out_specs=tok,
        out_shape=jax.ShapeDtypeStruct((b, l, d), _F32),
        compiler_params=_cparams("parallel", "parallel"),
        name="ffn",
    )(x, mod, g, w1, w2, g_final)


def kernel(x, c, ctx, c_ctx, w_mod, b_mod, g_norm1, g_norm2, w_in, w_br_a, w_br_b, w_br_c, w_out, s5_lam_re, s5_lam_im, s5_log_dt, s5_b_re, s5_b_im, s5_c_re, s5_c_im, s5_d, s5_w_glu, na_rpb, w_ff1, w_ff2, g_final):
    b, l, d = x.shape
    lc = ctx.shape[1]
    depth = w_mod.shape[0]
    fw = w_br_a.shape[1]
    sw = w_br_b.shape[1]
    aw = w_br_c.shape[1]
    off_g = fw + sw + 3 * aw
    tm = TOKEN_TILE
    tmc = lc

    c8 = jnp.concatenate([c, c_ctx[None, :], jnp.zeros((SUBLANES - b - 1, d), _F32)], axis=0)
    mod = _modulation(c8, w_mod, b_mod).reshape(depth, SUBLANES, 6, d)
    gf = g_final.reshape(1, d)
    g1 = g_norm1.reshape(depth, 1, d)
    g2 = g_norm2.reshape(depth, 1, d)
    d_skip = s5_d.reshape(depth, 1, sw)
    bf = lambda t: t.astype(_BF16)
    w_in_b, w_a, w_b, w_c, w_o, w_glu, w1, w2 = (bf(w_in), bf(w_br_a), bf(w_br_b), bf(w_br_c), bf(w_out),
                                                 bf(s5_w_glu), bf(w_ff1), bf(w_ff2))
    widths = (fw, sw, aw)

    for i in range(depth):
        need_ctx_out = i < depth - 1
        h_x, zf_x, zs_x, q_x, k_x, v_x = _inproj(x, mod, g1, w_in_b, i, None, widths, tm)
        h_c, zf_c, zs_c, q_c, k_c, v_c = _inproj(ctx, mod, g1, w_in_b, i, b, widths, tmc)

        fa_x = _fnet_latent(zf_x)
        (yf_x, yr_x), (yf_c, yr_c) = _s5_mix(zs_x, zs_c, b, s5_lam_re[i], s5_lam_im[i], s5_log_dt[i],
                                              s5_b_re[i], s5_b_im[i], s5_c_re[i], s5_c_im[i])
        nc_x = _na_latent(q_x, k_x, v_x, k_c, v_c, na_rpb[i])

        branch_w = (d_skip, w_glu, w_in_b, w_a, w_b, w_c, w_o)
        x = _merge(x, mod, h_x, fa_x, zs_x, yf_x, yr_x, nc_x, *branch_w, i, None, off_g, tm)
        x = _ffn(x, mod, g2, w1, w2, gf, not need_ctx_out, i, None, tm)

        if need_ctx_out:
            fa_c = _fnet_ctx(zf_c)
            nc_c = _ctx_attention(q_c, k_c, v_c)
            ctx = _merge(ctx, mod, h_c, fa_c, zs_c, yf_c, yr_c, nc_c, *branch_w, i, b, off_g, tmc)
            ctx = _ffn(ctx, mod, g2, w1, w2, gf, False, i, b, tmc)
    return x
```

```python
import functools
import math

import numpy as np
import jax
import jax.numpy as jnp
from jax import lax
from jax.experimental import pallas as pl
from jax.experimental.pallas import tpu as pltpu

_F32 = jnp.float32
_BF16 = jnp.bfloat16

RMS_EPS = 1e-6
GRID_W = 64
F_GROUP_DIM = 64
NA_HEAD_DIM = 64
NA_KH = 8
NA_KW = 16
NA_QROWS = 32
NEG_BIG = -1e30
FFT_L2 = 128
FFT_T2_STEP = 64
FFT_K1_STEP = 32
S5_TS = 64
TOKEN_TILE = 1024
FFN_CHUNK = 1024
MOD_TN = 1024

SUBLANES = 8
VMEM_LIMIT_BYTES = 56 * 1024 * 1024


def _cparams(*sem):
    return pltpu.CompilerParams(dimension_semantics=sem, vmem_limit_bytes=VMEM_LIMIT_BYTES)


def _dot(a, b):
    return jnp.dot(a, b, preferred_element_type=_F32)


def _table_bf16(t):
    return jnp.asarray(t, _F32).astype(_BF16)


def _dot_nt(a, b):
    return lax.dot_general(a, b, (((1,), (1,)), ((), ())), preferred_element_type=_F32)


def _norm_mod(x, g, shift, scale):
    ms = jnp.mean(x * x, axis=-1, keepdims=True)
    y = x * lax.rsqrt(ms + RMS_EPS) * g
    return y * (1.0 + scale) + shift


def _mod_kernel(c_ref, w_ref, b_ref, o_ref):
    c = c_ref[...]
    s = c * jax.nn.sigmoid(c)
    o_ref[...] = jnp.dot(s, w_ref[...], precision=lax.Precision.HIGHEST,
                         preferred_element_type=_F32) + b_ref[...]


def _modulation(c8, w_mod, b_mod):
    depth, d, n = w_mod.shape
    tn = MOD_TN
    rows = c8.shape[0]
    return pl.pallas_call(
        _mod_kernel,
        grid=(depth, n // tn),
        in_specs=[pl.BlockSpec((rows, d), lambda l, j: (0, 0)),
                  pl.BlockSpec((None, d, tn), lambda l, j: (l, 0, j)),
                  pl.BlockSpec((None, 1, tn), lambda l, j: (l, 0, j))],
        out_specs=pl.BlockSpec((None, rows, tn), lambda l, j: (l, 0, j)),
        out_shape=jax.ShapeDtypeStruct((depth, rows, n), _F32),
        compiler_params=_cparams("parallel", "parallel"),
        name="modulation",
    )(c8, w_mod, b_mod.reshape(depth, 1, n))


def _inproj_kernel(x_ref, mod_ref, g_ref, w_ref, h_ref, zf_ref, zs_ref, q_ref, k_ref, v_ref, *, widths, q_scale):
    h = _norm_mod(x_ref[...], g_ref[...], mod_ref[0:1, :], mod_ref[1:2, :]).astype(_BF16)
    h_ref[...] = h
    fw, sw, aw = widths
    o = 0
    zf_ref[...] = _dot(h, w_ref[:, o:o + fw]).astype(_BF16)
    o += fw
    zs_ref[...] = _dot(h, w_ref[:, o:o + sw]).astype(_BF16)
    o += sw
    q_ref[...] = (_dot(h, w_ref[:, o:o + aw]) * q_scale).astype(_BF16)
    o += aw
    k_ref[...] = _dot(h, w_ref[:, o:o + aw]).astype(_BF16)
    o += aw
    v_ref[...] = _dot(h, w_ref[:, o:o + aw]).astype(_BF16)


def _layer_specs(layer, mod_row, d):
    mod = pl.BlockSpec((None, None, 6, d), lambda bi, i: (layer, bi if mod_row is None else mod_row, 0, 0))
    mat = lambda a: pl.BlockSpec((None,) + a.shape[1:], lambda bi, i: (layer, 0, 0))
    return mod, mat


def _inproj(x, mod, g, w, layer, mod_row, widths, tm):
    b, l, d = x.shape
    fw, sw, aw = widths
    n = fw + sw + 3 * aw
    mod_spec, mat = _layer_specs(layer, mod_row, d)
    tok = lambda width: pl.BlockSpec((None, tm, width), lambda bi, i: (bi, i, 0))
    return pl.pallas_call(
        functools.partial(_inproj_kernel, widths=widths, q_scale=NA_HEAD_DIM ** -0.5),
        grid=(b, l // tm),
        in_specs=[tok(d), mod_spec, mat(g), pl.BlockSpec((None, d, n), lambda bi, i: (layer, 0, 0))],
        out_specs=[tok(d), tok(fw), pl.BlockSpec((tm, sw), lambda bi, i: (i, bi)), tok(aw), tok(aw), tok(aw)],
        out_shape=[jax.ShapeDtypeStruct((b, l, d), _BF16), jax.ShapeDtypeStruct((b, l, fw), _BF16), jax.ShapeDtypeStruct((l, b * sw), _BF16),
                   jax.ShapeDtypeStruct((b, l, aw), _BF16), jax.ShapeDtypeStruct((b, l, aw), _BF16),
                   jax.ShapeDtypeStruct((b, l, aw), _BF16)],
        compiler_params=_cparams("parallel", "parallel"),
        name="inproj",
    )(x, mod, g, w)


def _dft_tables(l):
    l1, l2 = l // FFT_L2, FFT_L2
    a1 = 2 * np.pi * np.outer(np.arange(l1), np.arange(l1)) / l1
    f1 = np.concatenate([np.cos(a1), np.sin(a1)], axis=0)
    at = 2 * np.pi * np.outer(np.arange(l1), np.arange(l2)) / l
    a2 = 2 * np.pi * np.outer(np.arange(l2), np.arange(l2)) / l2
    c2, s2 = np.cos(a2), np.sin(a2)
    f2 = np.block([[c2, -s2], [s2, c2]])
    return f1, np.cos(at), np.sin(at), f2


def _group_dft_tables(width):
    ac = 2 * np.pi * np.outer(np.arange(F_GROUP_DIM), np.arange(F_GROUP_DIM)) / F_GROUP_DIM
    eye = np.eye(width // F_GROUP_DIM)
    return np.kron(eye, np.cos(ac)), np.kron(eye, np.sin(ac))


def _fnet_a_kernel(x_ref, f1_ref, cw_ref, sw_ref, o_ref):
    a = _dot(f1_ref[...], x_ref[...])
    l1 = a.shape[0] // 2
    ac, asn = a[:l1], a[l1:]
    cw, sw = cw_ref[...], sw_ref[...]
    o_ref[0] = (ac * cw - asn * sw).astype(_BF16)
    o_ref[1] = (ac * sw + asn * cw).astype(_BF16)


def _fnet_b_kernel(a_ref, f2_ref, bc_ref, bs_ref, o_ref, *, scale, nk1, width):
    half = f2_ref.shape[0] // 2
    for j in range(nk1):
        rhs = jnp.concatenate([a_ref[0, j], a_ref[1, j]], axis=0)
        r = _dot(f2_ref[...], rhs)
        out = _dot(r[:half].astype(_BF16), bc_ref[...]) - _dot(r[half:].astype(_BF16), bs_ref[...])
        o_ref[:, j * width:(j + 1) * width] = (out * scale).astype(_BF16)


def _fnet_latent(zf):
    b, l, width = zf.shape
    l1, l2 = l // FFT_L2, FFT_L2
    f1, cw, sw, f2 = _dft_tables(l)
    bc, bs = _group_dft_tables(width)
    nc = FFT_T2_STEP * width
    cw = jnp.repeat(jnp.asarray(cw, _F32), width, axis=1)
    sw = jnp.repeat(jnp.asarray(sw, _F32), width, axis=1)
    a = pl.pallas_call(
        _fnet_a_kernel,
        grid=(l2 * width // nc, b),
        in_specs=[pl.BlockSpec((None, l1, nc), lambda j, bi: (bi, 0, j)),
                  pl.BlockSpec((2 * l1, l1), lambda j, bi: (0, 0)),
                  pl.BlockSpec((l1, nc), lambda j, bi: (0, j)),
                  pl.BlockSpec((l1, nc), lambda j, bi: (0, j))],
        out_specs=pl.BlockSpec((None, 2, l1, nc), lambda j, bi: (bi, 0, 0, j)),
        out_shape=jax.ShapeDtypeStruct((b, 2, l1, l2 * width), _BF16),
        compiler_params=_cparams("parallel", "parallel"),
        name="fnet_a",
    )(zf.reshape(b, l1, l2 * width), _table_bf16(f1), cw, sw)
    nk1 = FFT_K1_STEP
    out = pl.pallas_call(
        functools.partial(_fnet_b_kernel, scale=float((l * F_GROUP_DIM) ** -0.5), nk1=nk1, width=width),
        grid=(b, l1 // nk1),
        in_specs=[pl.BlockSpec((None, 2, nk1, l2, width), lambda bi, j: (bi, 0, j, 0, 0)),
                  pl.BlockSpec((2 * l2, 2 * l2), lambda bi, j: (0, 0)),
                  pl.BlockSpec((width, width), lambda bi, j: (0, 0)),
                  pl.BlockSpec((width, width), lambda bi, j: (0, 0))],
        out_specs=pl.BlockSpec((None, l2, nk1 * width), lambda bi, j: (bi, 0, j)),
        out_shape=jax.ShapeDtypeStruct((b, l2, l1 * width), _BF16),
        compiler_params=_cparams("parallel", "parallel"),
        name="fnet_b",
    )(a.reshape(b, 2, l1, l2, width), _table_bf16(f2), _table_bf16(bc), _table_bf16(bs))
    return out.reshape(b, l, width)


def _fnet_ctx_kernel(z_ref, cl_ref, sl_ref, bc_ref, bs_ref, o_ref, *, scale):
    z = z_ref[...]
    zc = _dot(z, bc_ref[...]).astype(_BF16)
    zs = _dot(z, bs_ref[...]).astype(_BF16)
    o_ref[...] = ((_dot(cl_ref[...], zc) - _dot(sl_ref[...], zs)) * scale).astype(_BF16)


def _fnet_ctx(zf):
    b, l, width = zf.shape
    al = 2 * np.pi * np.outer(np.arange(l), np.arange(l)) / l
    bc, bs = _group_dft_tables(width)
    full = lambda r, c: pl.BlockSpec((r, c), lambda bi: (0, 0))
    return pl.pallas_call(
        functools.partial(_fnet_ctx_kernel, scale=float((l * F_GROUP_DIM) ** -0.5)),
        grid=(b,),
        in_specs=[pl.BlockSpec((None, l, width), lambda bi: (bi, 0, 0)),
                  full(l, l), full(l, l), full(width, width), full(width, width)],
        out_specs=pl.BlockSpec((None, l, width), lambda bi: (bi, 0, 0)),
        out_shape=jax.ShapeDtypeStruct((b, l, width), _BF16),
        compiler_params=_cparams("parallel"),
        name="fnet_ctx",
    )(zf, _table_bf16(np.cos(al)), _table_bf16(np.sin(al)), _table_bf16(bc), _table_bf16(bs))


def _s5_disc_kernel(lr_ref, li_ref, ldt_ref, br_ref, bi_ref, lbr_ref, lbi_ref, bbr_ref, bbi_ref):
    lr = jnp.minimum(lr_ref[...], -1e-4)
    li = li_ref[...]
    dt = jnp.exp(ldt_ref[...])
    mag = jnp.exp(lr * dt)
    lbr = mag * jnp.cos(li * dt)
    lbi = mag * jnp.sin(li * dt)
    lbr_ref[...] = lbr
    lbi_ref[...] = lbi
    a, bb = lbr - 1.0, lbi
    den = lr * lr + li * li
    cr = (a * lr + bb * li) / den
    ci = (bb * lr - a * li) / den
    br, bi = br_ref[...], bi_ref[...]
    bbr_ref[...] = cr * br - ci * bi
    bbi_ref[...] = cr * bi + ci * br


def _s5_discretise(lam_re, lam_im, log_dt, b_re, b_im):
    _, g, p, n = b_re.shape
    rows = 2 * g * n
    ex = lambda t: jnp.broadcast_to(t[:, :, None, :], (2, g, n, p)).reshape(rows, p)
    ldt = jnp.broadcast_to(log_dt[:, :, None, None], (2, g, n, p)).reshape(rows, p)
    tb = lambda t: t.transpose(0, 1, 3, 2).reshape(rows, p)
    spec = pl.BlockSpec((rows, p), lambda: (0, 0))
    sds = jax.ShapeDtypeStruct((rows, p), _F32)
    lbr, lbi, bbr, bbi = pl.pallas_call(
        _s5_disc_kernel,
        in_specs=[spec] * 5, out_specs=[spec] * 4, out_shape=[sds] * 4,
        name="s5_discretise",
    )(ex(lam_re), ex(lam_im), ldt, tb(b_re), tb(b_im))
    shp = (2, g, n, p)
    return lbr.reshape(shp)[:, :, 0, :], lbi.reshape(shp)[:, :, 0, :], bbr.reshape(shp), bbi.reshape(shp)


def _s5_scan_kernel(ucf_ref, uxf_ref, ucr_ref, uxr_ref, perm_ref, permt_ref, bf_ref, br_ref, l_re_ref, l_im_ref,
                    cf_re_ref, cf_im_ref, cr_re_ref, cr_im_ref,
                    yfc_ref, yfx_ref, yrc_ref, yrx_ref,
                    d0a, d0b, d0c, d0d, d1a, d1b, d1c, d1d, h0a, h0b, h0c, h0d, h1a, h1b, h1c, h1d,
                    st_re, st_im, *, ts, nb, nc2, nst, sw):
    g = pl.program_id(0)
    d0, d1 = (d0a, d0b, d0c, d0d), (d1a, d1b, d1c, d1d)
    h0, h1 = (h0a, h0b, h0c, h0d), (h1a, h1b, h1c, h1d)

    @pl.when(g == 0)
    def _():
        for r in d0 + d1 + h0 + h1 + (st_re, st_im):
            r[...] = jnp.zeros_like(r)

    rt = SUBLANES
    ntile = ts // 2
    lower = lax.broadcasted_iota(jnp.int32, (rt, nst), 0) < nb
    is_ctx_in = g < nc2
    u_f = jnp.where(is_ctx_in, ucf_ref[...], uxf_ref[...])
    u_r = jnp.where(is_ctx_in, ucr_ref[...], uxr_ref[...])

    def time_batch_rows(u):
        stack = jnp.concatenate([u[:, b * sw:(b + 1) * sw] for b in range(nb)], axis=0)
        return _dot(perm_ref[...], stack).astype(_BF16)

    def drive(uf, ur, d):
        uf, ur = time_batch_rows(uf), time_batch_rows(ur)
        d[0][...] = _dot(uf, bf_ref[:, 0:nst])
        d[1][...] = _dot(uf, bf_ref[:, nst:2 * nst])
        d[2][...] = _dot(ur, br_ref[:, 0:nst])
        d[3][...] = _dot(ur, br_ref[:, nst:2 * nst])

    def scan(d, h):
        l_re, l_im = l_re_ref[...], l_im_ref[...]
        x_re, x_im = st_re[...], st_im[...]

        def step(x_re, x_im, s_re, s_im):
            return l_re * x_re - l_im * x_im + s_re, l_re * x_im + l_im * x_re + s_im

        for k in range(ntile // 2):
            f_tiles, r_tiles = [], []
            for m in (2 * k, 2 * k + 1):
                tf = slice(rt * m, rt * (m + 1))
                tr = slice(rt * (ntile - 1 - m), rt * (ntile - m))
                a_re, a_im, b_re, b_im = d[0][tf, :], d[1][tf, :], d[2][tr, :], d[3][tr, :]
                e_re, e_im = step(x_re, x_im, jnp.where(lower, a_re, b_re), jnp.where(lower, a_im, b_im))
                x_re, x_im = step(e_re, e_im, pltpu.roll(jnp.where(lower, b_re, a_re), nb, 0),
                                  pltpu.roll(jnp.where(lower, b_im, a_im), nb, 0))
                o_re, o_im = pltpu.roll(x_re, nb, 0), pltpu.roll(x_im, nb, 0)
                f_tiles.append((jnp.where(lower, e_re, o_re), jnp.where(lower, e_im, o_im)))
                r_tiles.append((jnp.where(lower, o_re, e_re), jnp.where(lower, o_im, e_im)))
            pf = slice(2 * rt * k, 2 * rt * (k + 1))
            pr = slice(2 * rt * (ntile // 2 - 1 - k), 2 * rt * (ntile // 2 - k))
            pack = lambda lo_t, hi_t: jnp.concatenate([lo_t, hi_t], axis=0).astype(_BF16)
            h[0][pf, :] = pack(f_tiles[0][0], f_tiles[1][0])
            h[1][pf, :] = pack(f_tiles[0][1], f_tiles[1][1])
            h[2][pr, :] = pack(r_tiles[1][0], r_tiles[0][0])
            h[3][pr, :] = pack(r_tiles[1][1], r_tiles[0][1])
        st_re[...] = x_re
        st_im[...] = x_im

    def readout(h):
        yf = (_dot(h[0][...], cf_re_ref[...]) - _dot(h[1][...], cf_im_ref[...])).astype(_BF16)
        yr = (_dot(h[2][...], cr_re_ref[...]) - _dot(h[3][...], cr_im_ref[...])).astype(_BF16)
        return _dot(permt_ref[...], yf).astype(_BF16), _dot(permt_ref[...], yr).astype(_BF16)

    lo, hi = slice(0, ts), slice(ts, 2 * ts)
    scan(d1, h1)
    drive(u_f[lo], u_r[hi], d0)
    yf0, yr0 = readout(h0)
    drive(u_f[hi], u_r[lo], d1)
    scan(d0, h0)
    yf1, yr1 = readout(h1)

    def store(yf_ref, yr_ref):
        for b in range(nb):
            cs = slice(b * sw, (b + 1) * sw)
            rows = slice(b * ts, (b + 1) * ts)
            yf_ref[lo, cs] = yf0[rows]
            yf_ref[hi, cs] = yf1[rows]
            yr_ref[hi, cs] = yr0[rows]
            yr_ref[lo, cs] = yr1[rows]

    pl.when(jnp.logical_and(g >= 1, g <= nc2))(lambda: store(yfc_ref, yrc_ref))
    pl.when(g > nc2)(lambda: store(yfx_ref, yrx_ref))


def _s5_scan(u_c, u_x, bf, br, l_re, l_im, cf_re, cf_im, cr_re, cr_im, nb, ts):
    sw = u_x.shape[1] // nb
    nst = l_re.shape[1]
    blk = 2 * ts
    nc2, nx2 = u_c.shape[0] // blk, u_x.shape[0] // blk
    nt2 = nc2 + nx2
    perm = np.zeros((ts * nb, ts * nb), np.float32)
    for t in range(ts):
        for b in range(nb):
            perm[t * nb + b, b * ts + t] = 1.0
    full = lambda a: pl.BlockSpec(a.shape, lambda g: (0, 0))
    spec = lambda f: pl.BlockSpec((blk, nb * sw), lambda g: (f(g), 0))
    clip = lambda v, n: jnp.clip(v, 0, n - 1)
    ins = [spec(lambda g: clip(g, nc2)), spec(lambda g: clip(g - nc2, nx2)),
           spec(lambda g: clip(nc2 - 1 - g, nc2)), spec(lambda g: clip(nt2 - 1 - g, nx2))]
    outs = [spec(lambda g: clip(g - 1, nc2)), spec(lambda g: clip(g - 1 - nc2, nx2)),
            spec(lambda g: clip(nc2 - g, nc2)), spec(lambda g: clip(nt2 - g, nx2))]
    perm_b, permt_b = _table_bf16(perm), _table_bf16(perm.T)
    sds = lambda a: jax.ShapeDtypeStruct(a.shape, _BF16)
    return pl.pallas_call(
        functools.partial(_s5_scan_kernel, ts=ts, nb=nb, nc2=nc2, nst=nst, sw=sw),
        grid=(nt2 + 1,),
        in_specs=ins + [full(perm_b), full(permt_b), full(bf), full(br), full(l_re), full(l_im),
                        full(cf_re), full(cf_im), full(cr_re), full(cr_im)],
        out_specs=outs,
        out_shape=[sds(u_c), sds(u_x), sds(u_c), sds(u_x)],
        scratch_shapes=([pltpu.VMEM((ts * nb, nst), _F32)] * 8 + [pltpu.VMEM((ts * nb, nst), _BF16)] * 8
                        + [pltpu.VMEM((SUBLANES, nst), _F32)] * 2),
        compiler_params=_cparams("arbitrary"),
        name="s5_scan",
    )(u_c, u_x, u_c, u_x, perm_b, permt_b, bf, br, l_re, l_im, cf_re, cf_im, cr_re, cr_im)


def _block_diag(t):
    g, r, c = t.shape
    eye = jnp.eye(g, dtype=t.dtype)
    return (t[:, :, None, :] * eye[:, None, :, None]).reshape(g * r, g * c)


def _s5_mix(zs_x, zs_c, nb, lam_re, lam_im, log_dt, b_re, b_im, c_re, c_im):
    assert 2 * nb == SUBLANES, "the scan state tile packs batch x direction on the sublanes"
    lbr, lbi, bbr, bbi = _s5_discretise(lam_re, lam_im, log_dt, b_re, b_im)
    nst = lbr.shape[1] * lbr.shape[2]
    drive = lambda d: jnp.concatenate([_block_diag(bbr[d]), _block_diag(bbi[d])], axis=1).astype(_BF16)
    rd = lambda t: _block_diag(t.transpose(0, 2, 1)).astype(_BF16)
    tile = lambda t: jnp.concatenate([jnp.broadcast_to(t[0].reshape(1, nst), (nb, nst)),
                                      jnp.broadcast_to(t[1].reshape(1, nst), (nb, nst))], axis=0)
    yf_c, yf_x, yr_c, yr_x = _s5_scan(zs_c, zs_x, drive(0), drive(1), tile(lbr), tile(lbi),
                                      rd(c_re[0]), rd(c_im[0]), rd(c_re[1]), rd(c_im[1]), nb, S5_TS)
    return (yf_x, yr_x), (yf_c, yr_c)


def _na_row_entries(rows):
    r = np.concatenate([np.arange(NA_QROWS), [NA_QROWS], rows - NA_QROWS + np.arange(NA_QROWS)])
    rs = np.clip(r - NA_KH // 2, 0, rows - NA_KH)
    idx = rs[:, None] + np.arange(NA_KH)[None, :] - r[:, None] + NA_KH - 1
    return tuple(tuple(int(v) for v in row) for row in idx)


def _na_table_kernel(rpb_ref, sel_ref, valid_ref, o_ref):
    t = jnp.dot(rpb_ref[...], sel_ref[...], precision=lax.Precision.HIGHEST, preferred_element_type=_F32)
    o_ref[...] = jnp.where(valid_ref[...] > 0.0, t, NEG_BIG)


def _na_column_tables(rpb):
    h, nr, ncol = rpb.shape
    w = np.arange(GRID_W)
    kc = np.arange(GRID_W)
    cs = np.clip(w - NA_KW // 2, 0, GRID_W - NA_KW)
    valid_c = (kc[None, :] >= cs[:, None]) & (kc[None, :] < cs[:, None] + NA_KW)
    idx_c = np.clip(kc[None, :] - w[:, None] + NA_KW - 1, 0, ncol - 1).reshape(-1)
    kpad = -(-ncol // SUBLANES) * SUBLANES
    sel = np.zeros((kpad, GRID_W * GRID_W), np.float32)
    sel[idx_c, np.arange(GRID_W * GRID_W)] = 1.0
    full = lambda r, c: pl.BlockSpec((r, c), lambda: (0, 0))
    out = pl.pallas_call(
        _na_table_kernel,
        in_specs=[full(h * nr, kpad), full(kpad, GRID_W * GRID_W), full(1, GRID_W * GRID_W)],
        out_specs=full(h * nr, GRID_W * GRID_W),
        out_shape=jax.ShapeDtypeStruct((h * nr, GRID_W * GRID_W), _F32),
        name="na_tables",
    )(jnp.pad(rpb.reshape(h * nr, ncol), ((0, 0), (0, kpad - ncol))), jnp.asarray(sel),
      jnp.asarray(valid_c.reshape(1, -1).astype(np.float32)))
    return out.reshape(h, nr, GRID_W, GRID_W)


def _na_kernel(q_ref, k_ref, v_ref, kc_ref, vc_ref, t_ref, o_ref, tab_ref, *, rows, nb, row_entries):
    blk = pl.program_id(2)
    dh = NA_HEAD_DIM
    win = NA_KH * GRID_W
    gw = GRID_W

    @pl.when(jnp.logical_and(pl.program_id(1) == 0, blk == 0))
    def _():
        for e, idx in enumerate(row_entries):
            for h in range(2):
                tab_ref[e, h * gw:(h + 1) * gw, :] = jnp.concatenate([t_ref[h, a] for a in idx], axis=1)

    q = q_ref[...]
    head0 = lax.broadcasted_iota(jnp.int32, q.shape, 1) < dh
    zero = jnp.zeros_like(q)
    q0, q1 = jnp.where(head0, q, zero), jnp.where(head0, zero, q)
    qq = jnp.concatenate([t[dr * gw:(dr + 1) * gw] for dr in range(NA_QROWS) for t in (q0, q1)], axis=0)
    starts, entries = [], []
    for dr in range(NA_QROWS):
        r = blk * NA_QROWS + dr
        rs = jnp.clip(r - NA_KH // 2, 0, rows - NA_KH)
        starts.append(pl.multiple_of(rs * gw, gw))
        entries.append(jnp.where(blk == 0, dr, jnp.where(blk == nb - 1, NA_QROWS + 1 + dr, NA_QROWS)))
    pair = lambda dr: slice(2 * dr * gw, 2 * (dr + 1) * gw)
    s_ctx = _dot_nt(qq, kc_ref[...])
    s_loc = jnp.concatenate(
        [_dot_nt(qq[pair(dr)], k_ref[pl.ds(starts[dr], win), :]) + tab_ref[entries[dr]]
         for dr in range(NA_QROWS)], axis=0)
    m = jnp.maximum(jnp.max(s_loc, axis=-1, keepdims=True), jnp.max(s_ctx, axis=-1, keepdims=True))
    p_loc = jnp.exp(s_loc - m)
    p_ctx = jnp.exp(s_ctx - m)
    den = jnp.sum(p_loc, axis=-1, keepdims=True) + jnp.sum(p_ctx, axis=-1, keepdims=True)
    p_loc = p_loc.astype(_BF16)
    o = jnp.concatenate(
        [_dot(p_loc[pair(dr)], v_ref[pl.ds(starts[dr], win), :]) for dr in range(NA_QROWS)], axis=0)
    o = (o + _dot(p_ctx.astype(_BF16), vc_ref[...])) / den
    h0_lanes = lax.broadcasted_iota(jnp.int32, (gw, q.shape[1]), 1) < dh
    for dr in range(NA_QROWS):
        o_ref[dr * gw:(dr + 1) * gw, :] = jnp.where(
            h0_lanes, o[2 * dr * gw:(2 * dr + 1) * gw], o[(2 * dr + 1) * gw:(2 * dr + 2) * gw]).astype(_BF16)


def _na_latent(q, k, v, kc, vc, rpb):
    b, l, aw = q.shape
    lc = kc.shape[1]
    rows = l // GRID_W
    assert rows % NA_QROWS == 0 and rows >= 2 * NA_QROWS
    nb = rows // NA_QROWS
    hp = aw // (2 * NA_HEAD_DIM)
    tq = NA_QROWS * GRID_W
    row_entries = _na_row_entries(rows)
    tables = _na_column_tables(rpb.astype(_F32))
    tables = tables.reshape((hp, 2) + tables.shape[1:])
    return pl.pallas_call(
        functools.partial(_na_kernel, rows=rows, nb=nb, row_entries=row_entries),
        grid=(hp, b, nb),
        in_specs=[pl.BlockSpec((None, tq, 2 * NA_HEAD_DIM), lambda p, bi, j: (bi, j, p)),
                  pl.BlockSpec((None, l, 2 * NA_HEAD_DIM), lambda p, bi, j: (bi, 0, p)),
                  pl.BlockSpec((None, l, 2 * NA_HEAD_DIM), lambda p, bi, j: (bi, 0, p)),
                  pl.BlockSpec((None, lc, 2 * NA_HEAD_DIM), lambda p, bi, j: (bi, 0, p)),
                  pl.BlockSpec((None, lc, 2 * NA_HEAD_DIM), lambda p, bi, j: (bi, 0, p)),
                  pl.BlockSpec((None,) + tables.shape[1:], lambda p, bi, j: (p, 0, 0, 0, 0))],
        out_specs=pl.BlockSpec((None, tq, 2 * NA_HEAD_DIM), lambda p, bi, j: (bi, j, p)),
        out_shape=jax.ShapeDtypeStruct((b, l, aw), _BF16),
        scratch_shapes=[pltpu.VMEM((len(row_entries), 2 * GRID_W, NA_KH * GRID_W), _F32)],
        compiler_params=_cparams("arbitrary", "arbitrary", "arbitrary"),
        name="na_latent",
    )(q, k, v, kc, vc, tables)


def _ctx_attn_kernel(q_ref, k_ref, v_ref, o_ref):
    dh = NA_HEAD_DIM
    outs = []
    for h in range(2):
        ls = slice(h * dh, (h + 1) * dh)
        s = _dot_nt(q_ref[:, ls], k_ref[:, ls])
        m = jnp.max(s, axis=-1, keepdims=True)
        p = jnp.exp(s - m)
        den = jnp.sum(p, axis=-1, keepdims=True)
        outs.append(_dot(p.astype(_BF16), v_ref[:, ls]) / den)
    o_ref[...] = jnp.concatenate(outs, axis=-1).astype(_BF16)


def _ctx_attention(q, k, v):
    b, lc, aw = q.shape
    hp = aw // (2 * NA_HEAD_DIM)
    spec = pl.BlockSpec((None, lc, 2 * NA_HEAD_DIM), lambda bi, p: (bi, 0, p))
    return pl.pallas_call(
        _ctx_attn_kernel,
        grid=(b, hp),
        in_specs=[spec] * 3, out_specs=spec,
        out_shape=jax.ShapeDtypeStruct((b, lc, aw), _BF16),
        compiler_params=_cparams("parallel", "parallel"),
        name="ctx_attention",
    )(q, k, v)


def _sigmoid(z):
    return 0.5 * jnp.tanh(0.5 * z) + 0.5


def _gelu_tanh(x):
    return 0.5 * x * (1.0 + jnp.tanh(math.sqrt(2.0 / math.pi) * (x + 0.044715 * (x * x * x))))


def _merge_kernel(x_ref, mod_ref, h_ref, fa_ref, u_ref, yf_ref, yr_ref, nc_ref, d_ref, wglu_ref,
                  wga_ref, wgb_ref, wgc_ref, wa_ref, wb_ref, wc_ref, wo_ref, o_ref):
    x = x_ref[...]
    h = h_ref[...]
    y = u_ref[...].astype(_F32) * d_ref[...] + yf_ref[...].astype(_F32) + yr_ref[...].astype(_F32)
    gl = _gelu_tanh(y)
    sb = gl * _sigmoid(_dot(gl.astype(_BF16), wglu_ref[...]))
    m = _sigmoid(_dot(h, wga_ref[...])) * _dot(fa_ref[...], wa_ref[...])
    m += _sigmoid(_dot(h, wgb_ref[...])) * _dot(sb.astype(_BF16), wb_ref[...])
    m += _sigmoid(_dot(h, wgc_ref[...])) * _dot(nc_ref[...], wc_ref[...])
    o_ref[...] = x + mod_ref[2:3, :] * _dot(m.astype(_BF16), wo_ref[...])


def _merge(x, mod, h, fa, u, yf, yr, nc, d_skip, w_glu, w_in, w_a, w_b, w_c, w_out, layer, mod_row, off_g, tm):
    b, l, d = x.shape
    assert off_g % d == 0
    mod_spec, mat = _layer_specs(layer, mod_row, d)
    gate = lambda k: pl.BlockSpec((None, d, d), lambda bi, i: (layer, 0, off_g // d + k))
    tok = lambda a: pl.BlockSpec((None, tm, a.shape[2]), lambda bi, i: (bi, i, 0))
    tmaj = lambda a: pl.BlockSpec((tm, a.shape[1] // b), lambda bi, i: (i, bi))
    return pl.pallas_call(
        _merge_kernel,
        grid=(b, l // tm),
        in_specs=[tok(x), mod_spec, tok(h), tok(fa), tmaj(u), tmaj(yf), tmaj(yr), tok(nc), mat(d_skip), mat(w_glu),
                  gate(0), gate(1), gate(2), mat(w_a), mat(w_b), mat(w_c), mat(w_out)],
        out_specs=tok(x),
        out_shape=jax.ShapeDtypeStruct((b, l, d), _F32),
        compiler_params=_cparams("parallel", "parallel"),
        name="merge",
    )(x, mod, h, fa, u, yf, yr, nc, d_skip, w_glu, w_in, w_in, w_in, w_a, w_b, w_c, w_out)


def _ffn_kernel(x_ref, mod_ref, g_ref, w1_ref, w2_ref, gf_ref, o_ref, *, final_norm, fchunk):
    x = x_ref[...]
    h = _norm_mod(x, g_ref[...], mod_ref[3:4, :], mod_ref[4:5, :]).astype(_BF16)
    dff = w1_ref.shape[1]
    acc = jnp.zeros_like(x)
    for c in range(dff // fchunk):
        cs = slice(c * fchunk, (c + 1) * fchunk)
        t = jnp.maximum(_dot(h, w1_ref[:, cs]), 0.0)
        acc += _dot((t * t).astype(_BF16), w2_ref[cs, :])
    x2 = x + mod_ref[5:6, :] * acc
    if final_norm:
        ms = jnp.mean(x2 * x2, axis=-1, keepdims=True)
        x2 = x2 * lax.rsqrt(ms + RMS_EPS) * gf_ref[...]
    o_ref[...] = x2


def _ffn(x, mod, g, w1, w2, g_final, final_norm, layer, mod_row, tm):
    b, l, d = x.shape
    mod_spec, mat = _layer_specs(layer, mod_row, d)
    tok = pl.BlockSpec((None, tm, d), lambda bi, i: (bi, i, 0))
    full = lambda a: pl.BlockSpec(a.shape, lambda bi, i: (0, 0))
    return pl.pallas_call(
        functools.partial(_ffn_kernel, final_norm=final_norm, fchunk=FFN_CHUNK),
        grid=(b, l // tm),
        in_specs=[tok, mod_spec, mat(g), mat(w1), mat(w2), full(g_final)],
        out_specs=tok,
        out_shape=jax.ShapeDtypeStruct((b, l, d), _F32),
        compiler_params=_cparams("parallel", "parallel"),
        name="ffn",
    )(x, mod, g, w1, w2, g_final)


def kernel(x, c, ctx, c_ctx, w_mod, b_mod, g_norm1, g_norm2, w_in, w_br_a, w_br_b, w_br_c, w_out, s5_lam_re, s5_lam_im, s5_log_dt, s5_b_re, s5_b_im, s5_c_re, s5_c_im, s5_d, s5_w_glu, na_rpb, w_ff1, w_ff2, g_final):
    b, l, d = x.shape
    lc = ctx.shape[1]
    depth = w_mod.shape[0]
    fw = w_br_a.shape[1]
    sw = w_br_b.shape[1]
    aw = w_br_c.shape[1]
    off_g = fw + sw + 3 * aw
    tm = TOKEN_TILE
    tmc = lc

    c8 = jnp.concatenate([c, c_ctx[None, :], jnp.zeros((SUBLANES - b - 1, d), _F32)], axis=0)
    mod = _modulation(c8, w_mod, b_mod).reshape(depth, SUBLANES, 6, d)
    gf = g_final.reshape(1, d)
    g1 = g_norm1.reshape(depth, 1, d)
    g2 = g_norm2.reshape(depth, 1, d)
    d_skip = s5_d.reshape(depth, 1, sw)
    bf = lambda t: t.astype(_BF16)
    w_in_b, w_a, w_b, w_c, w_o, w_glu, w1, w2 = (bf(w_in), bf(w_br_a), bf(w_br_b), bf(w_br_c), bf(w_out),
                                                 bf(s5_w_glu), bf(w_ff1), bf(w_ff2))
    widths = (fw, sw, aw)

    for i in range(depth):
        need_ctx_out = i < depth - 1
        h_x, zf_x, zs_x, q_x, k_x, v_x = _inproj(x, mod, g1, w_in_b, i, None, widths, tm)
        h_c, zf_c, zs_c, q_c, k_c, v_c = _inproj(ctx, mod, g1, w_in_b, i, b, widths, tmc)

        fa_x = _fnet_latent(zf_x)
        (yf_x, yr_x), (yf_c, yr_c) = _s5_mix(zs_x, zs_c, b, s5_lam_re[i], s5_lam_im[i], s5_log_dt[i],
                                              s5_b_re[i], s5_b_im[i], s5_c_re[i], s5_c_im[i])
        nc_x = _na_latent(q_x, k_x, v_x, k_c, v_c, na_rpb[i])

        branch_w = (d_skip, w_glu, w_in_b, w_a, w_b, w_c, w_o)
        x = _merge(x, mod, h_x, fa_x, zs_x, yf_x, yr_x, nc_x, *branch_w, i, None, off_g, tm)
        x = _ffn(x, mod, g2, w1, w2, gf, not need_ctx_out, i, None, tm)

        if need_ctx_out:
            fa_c = _fnet_ctx(zf_c)
            nc_c = _ctx_attention(q_c, k_c, v_c)
            ctx = _merge(ctx, mod, h_c, fa_c, zs_c, yf_c, yr_c, nc_c, *branch_w, i, b, off_g, tmc)
            ctx = _ffn(ctx, mod, g2, w1, w2, gf, False, i, b, tmc)
    return x
```

```python
import functools
import math

import numpy as np
import jax
import jax.numpy as jnp
from jax import lax
from jax.experimental import pallas as pl
from jax.experimental.pallas import tpu as pltpu

_F32 = jnp.float32
_BF16 = jnp.bfloat16

RMS_EPS = 1e-6
GRID_W = 64
F_GROUP_DIM = 64
NA_HEAD_DIM = 64
NA_KH = 8
NA_KW = 16
NA_QROWS = 64
NEG_BIG = -1e30
FFT_L2 = 128
FFT_T2_STEP = 64
FFT_K1_STEP = 32
S5_TS = 64
TOKEN_TILE = 1024
FFN_CHUNK = 1024
MOD_TN = 1024

SUBLANES = 8
VMEM_LIMIT_BYTES = 56 * 1024 * 1024


def _cparams(*sem):
    return pltpu.CompilerParams(dimension_semantics=sem, vmem_limit_bytes=VMEM_LIMIT_BYTES)


def _dot(a, b):
    return jnp.dot(a, b, preferred_element_type=_F32)


def _table_bf16(t):
    return jnp.asarray(t, _F32).astype(_BF16)


def _dot_nt(a, b):
    return lax.dot_general(a, b, (((1,), (1,)), ((), ())), preferred_element_type=_F32)


def _norm_mod(x, g, shift, scale):
    ms = jnp.mean(x * x, axis=-1, keepdims=True)
    y = x * lax.rsqrt(ms + RMS_EPS) * g
    return y * (1.0 + scale) + shift


def _mod_kernel(c_ref, w_ref, b_ref, o_ref):
    c = c_ref[...]
    s = c * jax.nn.sigmoid(c)
    o_ref[...] = jnp.dot(s, w_ref[...], precision=lax.Precision.HIGHEST,
                         preferred_element_type=_F32) + b_ref[...]


def _modulation(c8, w_mod, b_mod):
    depth, d, n = w_mod.shape
    tn = MOD_TN
    rows = c8.shape[0]
    return pl.pallas_call(
        _mod_kernel,
        grid=(depth, n // tn),
        in_specs=[pl.BlockSpec((rows, d), lambda l, j: (0, 0)),
                  pl.BlockSpec((None, d, tn), lambda l, j: (l, 0, j)),
                  pl.BlockSpec((None, 1, tn), lambda l, j: (l, 0, j))],
        out_specs=pl.BlockSpec((None, rows, tn), lambda l, j: (l, 0, j)),
        out_shape=jax.ShapeDtypeStruct((depth, rows, n), _F32),
        compiler_params=_cparams("parallel", "parallel"),
        name="modulation",
    )(c8, w_mod, b_mod.reshape(depth, 1, n))


def _inproj_kernel(x_ref, mod_ref, g_ref, w_ref, h_ref, zf_ref, zs_ref, q_ref, k_ref, v_ref, *, widths, q_scale):
    h = _norm_mod(x_ref[...], g_ref[...], mod_ref[0:1, :], mod_ref[1:2, :]).astype(_BF16)
    h_ref[...] = h
    fw, sw, aw = widths
    o = 0
    zf_ref[...] = _dot(h, w_ref[:, o:o + fw]).astype(_BF16)
    o += fw
    zs_ref[...] = _dot(h, w_ref[:, o:o + sw]).astype(_BF16)
    o += sw
    q_ref[...] = (_dot(h, w_ref[:, o:o + aw]) * q_scale).astype(_BF16)
    o += aw
    k_ref[...] = _dot(h, w_ref[:, o:o + aw]).astype(_BF16)
    o += aw
    v_ref[...] = _dot(h, w_ref[:, o:o + aw]).astype(_BF16)


def _layer_specs(layer, mod_row, d):
    mod = pl.BlockSpec((None, None, 6, d), lambda bi, i: (layer, bi if mod_row is None else mod_row, 0, 0))
    mat = lambda a: pl.BlockSpec((None,) + a.shape[1:], lambda bi, i: (layer, 0, 0))
    return mod, mat


def _inproj(x, mod, g, w, layer, mod_row, widths, tm):
    b, l, d = x.shape
    fw, sw, aw = widths
    n = fw + sw + 3 * aw
    mod_spec, mat = _layer_specs(layer, mod_row, d)
    tok = lambda width: pl.BlockSpec((None, tm, width), lambda bi, i: (bi, i, 0))
    return pl.pallas_call(
        functools.partial(_inproj_kernel, widths=widths, q_scale=NA_HEAD_DIM ** -0.5),
        grid=(b, l // tm),
        in_specs=[tok(d), mod_spec, mat(g), pl.BlockSpec((None, d, n), lambda bi, i: (layer, 0, 0))],
        out_specs=[tok(d), tok(fw), pl.BlockSpec((tm, sw), lambda bi, i: (i, bi)), tok(aw), tok(aw), tok(aw)],
        out_shape=[jax.ShapeDtypeStruct((b, l, d), _BF16), jax.ShapeDtypeStruct((b, l, fw), _BF16), jax.ShapeDtypeStruct((l, b * sw), _BF16),
                   jax.ShapeDtypeStruct((b, l, aw), _BF16), jax.ShapeDtypeStruct((b, l, aw), _BF16),
                   jax.ShapeDtypeStruct((b, l, aw), _BF16)],
        compiler_params=_cparams("parallel", "parallel"),
        name="inproj",
    )(x, mod, g, w)


def _dft_tables(l):
    l1, l2 = l // FFT_L2, FFT_L2
    a1 = 2 * np.pi * np.outer(np.arange(l1), np.arange(l1)) / l1
    f1 = np.concatenate([np.cos(a1), np.sin(a1)], axis=0)
    at = 2 * np.pi * np.outer(np.arange(l1), np.arange(l2)) / l
    a2 = 2 * np.pi * np.outer(np.arange(l2), np.arange(l2)) / l2
    c2, s2 = np.cos(a2), np.sin(a2)
    f2 = np.block([[c2, -s2], [s2, c2]])
    return f1, np.cos(at), np.sin(at), f2


def _group_dft_tables(width):
    ac = 2 * np.pi * np.outer(np.arange(F_GROUP_DIM), np.arange(F_GROUP_DIM)) / F_GROUP_DIM
    eye = np.eye(width // F_GROUP_DIM)
    return np.kron(eye, np.cos(ac)), np.kron(eye, np.sin(ac))


def _fnet_a_kernel(x_ref, f1_ref, cw_ref, sw_ref, o_ref):
    a = _dot(f1_ref[...], x_ref[...])
    l1 = a.shape[0] // 2
    ac, asn = a[:l1], a[l1:]
    cw, sw = cw_ref[...], sw_ref[...]
    o_ref[0] = (ac * cw - asn * sw).astype(_BF16)
    o_ref[1] = (ac * sw + asn * cw).astype(_BF16)


def _fnet_b_kernel(a_ref, f2_ref, bc_ref, bs_ref, o_ref, *, scale, nk1, width):
    half = f2_ref.shape[0] // 2
    for j in range(nk1):
        rhs = jnp.concatenate([a_ref[0, j], a_ref[1, j]], axis=0)
        r = _dot(f2_ref[...], rhs)
        out = _dot(r[:half].astype(_BF16), bc_ref[...]) - _dot(r[half:].astype(_BF16), bs_ref[...])
        o_ref[:, j * width:(j + 1) * width] = (out * scale).astype(_BF16)


def _fnet_latent(zf):
    b, l, width = zf.shape
    l1, l2 = l // FFT_L2, FFT_L2
    f1, cw, sw, f2 = _dft_tables(l)
    bc, bs = _group_dft_tables(width)
    nc = FFT_T2_STEP * width
    cw = jnp.repeat(jnp.asarray(cw, _F32), width, axis=1)
    sw = jnp.repeat(jnp.asarray(sw, _F32), width, axis=1)
    a = pl.pallas_call(
        _fnet_a_kernel,
        grid=(l2 * width // nc, b),
        in_specs=[pl.BlockSpec((None, l1, nc), lambda j, bi: (bi, 0, j)),
                  pl.BlockSpec((2 * l1, l1), lambda j, bi: (0, 0)),
                  pl.BlockSpec((l1, nc), lambda j, bi: (0, j)),
                  pl.BlockSpec((l1, nc), lambda j, bi: (0, j))],
        out_specs=pl.BlockSpec((None, 2, l1, nc), lambda j, bi: (bi, 0, 0, j)),
        out_shape=jax.ShapeDtypeStruct((b, 2, l1, l2 * width), _BF16),
        compiler_params=_cparams("parallel", "parallel"),
        name="fnet_a",
    )(zf.reshape(b, l1, l2 * width), _table_bf16(f1), cw, sw)
    nk1 = FFT_K1_STEP
    out = pl.pallas_call(
        functools.partial(_fnet_b_kernel, scale=float((l * F_GROUP_DIM) ** -0.5), nk1=nk1, width=width),
        grid=(b, l1 // nk1),
        in_specs=[pl.BlockSpec((None, 2, nk1, l2, width), lambda bi, j: (bi, 0, j, 0, 0)),
                  pl.BlockSpec((2 * l2, 2 * l2), lambda bi, j: (0, 0)),
                  pl.BlockSpec((width, width), lambda bi, j: (0, 0)),
                  pl.BlockSpec((width, width), lambda bi, j: (0, 0))],
        out_specs=pl.BlockSpec((None, l2, nk1 * width), lambda bi, j: (bi, 0, j)),
        out_shape=jax.ShapeDtypeStruct((b, l2, l1 * width), _BF16),
        compiler_params=_cparams("parallel", "parallel"),
        name="fnet_b",
    )(a.reshape(b, 2, l1, l2, width), _table_bf16(f2), _table_bf16(bc), _table_bf16(bs))
    return out.reshape(b, l, width)


def _fnet_ctx_kernel(z_ref, cl_ref, sl_ref, bc_ref, bs_ref, o_ref, *, scale):
    z = z_ref[...]
    zc = _dot(z, bc_ref[...]).astype(_BF16)
    zs = _dot(z, bs_ref[...]).astype(_BF16)
    o_ref[...] = ((_dot(cl_ref[...], zc) - _dot(sl_ref[...], zs)) * scale).astype(_BF16)


def _fnet_ctx(zf):
    b, l, width = zf.shape
    al = 2 * np.pi * np.outer(np.arange(l), np.arange(l)) / l
    bc, bs = _group_dft_tables(width)
    full = lambda r, c: pl.BlockSpec((r, c), lambda bi: (0, 0))
    return pl.pallas_call(
        functools.partial(_fnet_ctx_kernel, scale=float((l * F_GROUP_DIM) ** -0.5)),
        grid=(b,),
        in_specs=[pl.BlockSpec((None, l, width), lambda bi: (bi, 0, 0)),
                  full(l, l), full(l, l), full(width, width), full(width, width)],
        out_specs=pl.BlockSpec((None, l, width), lambda bi: (bi, 0, 0)),
        out_shape=jax.ShapeDtypeStruct((b, l, width), _BF16),
        compiler_params=_cparams("parallel"),
        name="fnet_ctx",
    )(zf, _table_bf16(np.cos(al)), _table_bf16(np.sin(al)), _table_bf16(bc), _table_bf16(bs))


def _s5_disc_kernel(lr_ref, li_ref, ldt_ref, br_ref, bi_ref, lbr_ref, lbi_ref, bbr_ref, bbi_ref):
    lr = jnp.minimum(lr_ref[...], -1e-4)
    li = li_ref[...]
    dt = jnp.exp(ldt_ref[...])
    mag = jnp.exp(lr * dt)
    lbr = mag * jnp.cos(li * dt)
    lbi = mag * jnp.sin(li * dt)
    lbr_ref[...] = lbr
    lbi_ref[...] = lbi
    a, bb = lbr - 1.0, lbi
    den = lr * lr + li * li
    cr = (a * lr + bb * li) / den
    ci = (bb * lr - a * li) / den
    br, bi = br_ref[...], bi_ref[...]
    bbr_ref[...] = cr * br - ci * bi
    bbi_ref[...] = cr * bi + ci * br


def _s5_discretise(lam_re, lam_im, log_dt, b_re, b_im):
    _, g, p, n = b_re.shape
    rows = 2 * g * n
    ex = lambda t: jnp.broadcast_to(t[:, :, None, :], (2, g, n, p)).reshape(rows, p)
    ldt = jnp.broadcast_to(log_dt[:, :, None, None], (2, g, n, p)).reshape(rows, p)
    tb = lambda t: t.transpose(0, 1, 3, 2).reshape(rows, p)
    spec = pl.BlockSpec((rows, p), lambda: (0, 0))
    sds = jax.ShapeDtypeStruct((rows, p), _F32)
    lbr, lbi, bbr, bbi = pl.pallas_call(
        _s5_disc_kernel,
        in_specs=[spec] * 5, out_specs=[spec] * 4, out_shape=[sds] * 4,
        name="s5_discretise",
    )(ex(lam_re), ex(lam_im), ldt, tb(b_re), tb(b_im))
    shp = (2, g, n, p)
    return lbr.reshape(shp)[:, :, 0, :], lbi.reshape(shp)[:, :, 0, :], bbr.reshape(shp), bbi.reshape(shp)


def _s5_scan_kernel(ucf_ref, uxf_ref, ucr_ref, uxr_ref, perm_ref, permt_ref, bf_ref, br_ref, l_re_ref, l_im_ref,
                    cf_re_ref, cf_im_ref, cr_re_ref, cr_im_ref,
                    yfc_ref, yfx_ref, yrc_ref, yrx_ref,
                    d0a, d0b, d0c, d0d, d1a, d1b, d1c, d1d, h0a, h0b, h0c, h0d, h1a, h1b, h1c, h1d,
                    st_re, st_im, *, ts, nb, nc2, nst, sw):
    g = pl.program_id(0)
    d0, d1 = (d0a, d0b, d0c, d0d), (d1a, d1b, d1c, d1d)
    h0, h1 = (h0a, h0b, h0c, h0d), (h1a, h1b, h1c, h1d)

    @pl.when(g == 0)
    def _():
        for r in d0 + d1 + h0 + h1 + (st_re, st_im):
            r[...] = jnp.zeros_like(r)

    rt = SUBLANES
    ntile = ts // 2
    lower = lax.broadcasted_iota(jnp.int32, (rt, nst), 0) < nb
    is_ctx_in = g < nc2
    u_f = jnp.where(is_ctx_in, ucf_ref[...], uxf_ref[...])
    u_r = jnp.where(is_ctx_in, ucr_ref[...], uxr_ref[...])

    def time_batch_rows(u):
        stack = jnp.concatenate([u[:, b * sw:(b + 1) * sw] for b in range(nb)], axis=0)
        return _dot(perm_ref[...], stack).astype(_BF16)

    def drive(uf, ur, d):
        uf, ur = time_batch_rows(uf), time_batch_rows(ur)
        d[0][...] = _dot(uf, bf_ref[:, 0:nst])
        d[1][...] = _dot(uf, bf_ref[:, nst:2 * nst])
        d[2][...] = _dot(ur, br_ref[:, 0:nst])
        d[3][...] = _dot(ur, br_ref[:, nst:2 * nst])

    def scan(d, h):
        l_re, l_im = l_re_ref[...], l_im_ref[...]
        x_re, x_im = st_re[...], st_im[...]

        def step(x_re, x_im, s_re, s_im):
            return l_re * x_re - l_im * x_im + s_re, l_re * x_im + l_im * x_re + s_im

        for k in range(ntile // 2):
            f_tiles, r_tiles = [], []
            for m in (2 * k, 2 * k + 1):
                tf = slice(rt * m, rt * (m + 1))
                tr = slice(rt * (ntile - 1 - m), rt * (ntile - m))
                a_re, a_im, b_re, b_im = d[0][tf, :], d[1][tf, :], d[2][tr, :], d[3][tr, :]
                e_re, e_im = step(x_re, x_im, jnp.where(lower, a_re, b_re), jnp.where(lower, a_im, b_im))
                x_re, x_im = step(e_re, e_im, pltpu.roll(jnp.where(lower, b_re, a_re), nb, 0),
                                  pltpu.roll(jnp.where(lower, b_im, a_im), nb, 0))
                o_re, o_im = pltpu.roll(x_re, nb, 0), pltpu.roll(x_im, nb, 0)
                f_tiles.append((jnp.where(lower, e_re, o_re), jnp.where(lower, e_im, o_im)))
                r_tiles.append((jnp.where(lower, o_re, e_re), jnp.where(lower, o_im, e_im)))
            pf = slice(2 * rt * k, 2 * rt * (k + 1))
            pr = slice(2 * rt * (ntile // 2 - 1 - k), 2 * rt * (ntile // 2 - k))
            pack = lambda lo_t, hi_t: jnp.concatenate([lo_t, hi_t], axis=0).astype(_BF16)
            h[0][pf, :] = pack(f_tiles[0][0], f_tiles[1][0])
            h[1][pf, :] = pack(f_tiles[0][1], f_tiles[1][1])
            h[2][pr, :] = pack(r_tiles[1][0], r_tiles[0][0])
            h[3][pr, :] = pack(r_tiles[1][1], r_tiles[0][1])
        st_re[...] = x_re
        st_im[...] = x_im

    def readout(h):
        yf = (_dot(h[0][...], cf_re_ref[...]) - _dot(h[1][...], cf_im_ref[...])).astype(_BF16)
        yr = (_dot(h[2][...], cr_re_ref[...]) - _dot(h[3][...], cr_im_ref[...])).astype(_BF16)
        return _dot(permt_ref[...], yf).astype(_BF16), _dot(permt_ref[...], yr).astype(_BF16)

    lo, hi = slice(0, ts), slice(ts, 2 * ts)
    scan(d1, h1)
    drive(u_f[lo], u_r[hi], d0)
    yf0, yr0 = readout(h0)
    drive(u_f[hi], u_r[lo], d1)
    scan(d0, h0)
    yf1, yr1 = readout(h1)

    def store(yf_ref, yr_ref):
        for b in range(nb):
            cs = slice(b * sw, (b + 1) * sw)
            rows = slice(b * ts, (b + 1) * ts)
            yf_ref[lo, cs] = yf0[rows]
            yf_ref[hi, cs] = yf1[rows]
            yr_ref[hi, cs] = yr0[rows]
            yr_ref[lo, cs] = yr1[rows]

    pl.when(jnp.logical_and(g >= 1, g <= nc2))(lambda: store(yfc_ref, yrc_ref))
    pl.when(g > nc2)(lambda: store(yfx_ref, yrx_ref))


def _s5_scan(u_c, u_x, bf, br, l_re, l_im, cf_re, cf_im, cr_re, cr_im, nb, ts):
    sw = u_x.shape[1] // nb
    nst = l_re.shape[1]
    blk = 2 * ts
    nc2, nx2 = u_c.shape[0] // blk, u_x.shape[0] // blk
    nt2 = nc2 + nx2
    perm = np.zeros((ts * nb, ts * nb), np.float32)
    for t in range(ts):
        for b in range(nb):
            perm[t * nb + b, b * ts + t] = 1.0
    full = lambda a: pl.BlockSpec(a.shape, lambda g: (0, 0))
    spec = lambda f: pl.BlockSpec((blk, nb * sw), lambda g: (f(g), 0))
    clip = lambda v, n: jnp.clip(v, 0, n - 1)
    ins = [spec(lambda g: clip(g, nc2)), spec(lambda g: clip(g - nc2, nx2)),
           spec(lambda g: clip(nc2 - 1 - g, nc2)), spec(lambda g: clip(nt2 - 1 - g, nx2))]
    outs = [spec(lambda g: clip(g - 1, nc2)), spec(lambda g: clip(g - 1 - nc2, nx2)),
            spec(lambda g: clip(nc2 - g, nc2)), spec(lambda g: clip(nt2 - g, nx2))]
    perm_b, permt_b = _table_bf16(perm), _table_bf16(perm.T)
    sds = lambda a: jax.ShapeDtypeStruct(a.shape, _BF16)
    return pl.pallas_call(
        functools.partial(_s5_scan_kernel, ts=ts, nb=nb, nc2=nc2, nst=nst, sw=sw),
        grid=(nt2 + 1,),
        in_specs=ins + [full(perm_b), full(permt_b), full(bf), full(br), full(l_re), full(l_im),
                        full(cf_re), full(cf_im), full(cr_re), full(cr_im)],
        out_specs=outs,
        out_shape=[sds(u_c), sds(u_x), sds(u_c), sds(u_x)],
        scratch_shapes=([pltpu.VMEM((ts * nb, nst), _F32)] * 8 + [pltpu.VMEM((ts * nb, nst), _BF16)] * 8
                        + [pltpu.VMEM((SUBLANES, nst), _F32)] * 2),
        compiler_params=_cparams("arbitrary"),
        name="s5_scan",
    )(u_c, u_x, u_c, u_x, perm_b, permt_b, bf, br, l_re, l_im, cf_re, cf_im, cr_re, cr_im)


def _block_diag(t):
    g, r, c = t.shape
    eye = jnp.eye(g, dtype=t.dtype)
    return (t[:, :, None, :] * eye[:, None, :, None]).reshape(g * r, g * c)


def _s5_mix(zs_x, zs_c, nb, lam_re, lam_im, log_dt, b_re, b_im, c_re, c_im):
    assert 2 * nb == SUBLANES, "the scan state tile packs batch x direction on the sublanes"
    lbr, lbi, bbr, bbi = _s5_discretise(lam_re, lam_im, log_dt, b_re, b_im)
    nst = lbr.shape[1] * lbr.shape[2]
    drive = lambda d: jnp.concatenate([_block_diag(bbr[d]), _block_diag(bbi[d])], axis=1).astype(_BF16)
    rd = lambda t: _block_diag(t.transpose(0, 2, 1)).astype(_BF16)
    tile = lambda t: jnp.concatenate([jnp.broadcast_to(t[0].reshape(1, nst), (nb, nst)),
                                      jnp.broadcast_to(t[1].reshape(1, nst), (nb, nst))], axis=0)
    yf_c, yf_x, yr_c, yr_x = _s5_scan(zs_c, zs_x, drive(0), drive(1), tile(lbr), tile(lbi),
                                      rd(c_re[0]), rd(c_im[0]), rd(c_re[1]), rd(c_im[1]), nb, S5_TS)
    return (yf_x, yr_x), (yf_c, yr_c)


def _na_row_entries(rows):
    half = NA_KH // 2
    r = np.concatenate([np.arange(half), [half], rows - half + 1 + np.arange(half - 1)])
    rs = np.clip(r - NA_KH // 2, 0, rows - NA_KH)
    idx = rs[:, None] + np.arange(NA_KH)[None, :] - r[:, None] + NA_KH - 1
    return tuple(tuple(int(v) for v in row) for row in idx)


def _na_table_kernel(rpb_ref, sel_ref, valid_ref, o_ref):
    t = jnp.dot(rpb_ref[...], sel_ref[...], precision=lax.Precision.HIGHEST, preferred_element_type=_F32)
    o_ref[...] = jnp.where(valid_ref[...] > 0.0, t, NEG_BIG)


def _na_column_tables(rpb):
    h, nr, ncol = rpb.shape
    w = np.arange(GRID_W)
    kc = np.arange(GRID_W)
    cs = np.clip(w - NA_KW // 2, 0, GRID_W - NA_KW)
    valid_c = (kc[None, :] >= cs[:, None]) & (kc[None, :] < cs[:, None] + NA_KW)
    idx_c = np.clip(kc[None, :] - w[:, None] + NA_KW - 1, 0, ncol - 1).reshape(-1)
    kpad = -(-ncol // SUBLANES) * SUBLANES
    sel = np.zeros((kpad, GRID_W * GRID_W), np.float32)
    sel[idx_c, np.arange(GRID_W * GRID_W)] = 1.0
    full = lambda r, c: pl.BlockSpec((r, c), lambda: (0, 0))
    out = pl.pallas_call(
        _na_table_kernel,
        in_specs=[full(h * nr, kpad), full(kpad, GRID_W * GRID_W), full(1, GRID_W * GRID_W)],
        out_specs=full(h * nr, GRID_W * GRID_W),
        out_shape=jax.ShapeDtypeStruct((h * nr, GRID_W * GRID_W), _F32),
        name="na_tables",
    )(jnp.pad(rpb.reshape(h * nr, ncol), ((0, 0), (0, kpad - ncol))), jnp.asarray(sel),
      jnp.asarray(valid_c.reshape(1, -1).astype(np.float32)))
    return out.reshape(h, nr, GRID_W, GRID_W)


def _na_kernel(q_ref, k_ref, v_ref, kc_ref, vc_ref, t_ref, o_ref, tab_ref, *, rows, nb, row_entries):
    blk = pl.program_id(2)
    dh = NA_HEAD_DIM
    win = NA_KH * GRID_W
    gw = GRID_W

    @pl.when(jnp.logical_and(pl.program_id(1) == 0, blk == 0))
    def _():
        for e, idx in enumerate(row_entries):
            for h in range(2):
                tab_ref[e, h * gw:(h + 1) * gw, :] = jnp.concatenate([t_ref[h, a] for a in idx], axis=1)

    q = q_ref[...]
    head0 = lax.broadcasted_iota(jnp.int32, q.shape, 1) < dh
    zero = jnp.zeros_like(q)
    q0, q1 = jnp.where(head0, q, zero), jnp.where(head0, zero, q)
    qq = jnp.concatenate([t[dr * gw:(dr + 1) * gw] for dr in range(NA_QROWS) for t in (q0, q1)], axis=0)
    starts, entries = [], []
    for dr in range(NA_QROWS):
        r = blk * NA_QROWS + dr
        rs = jnp.clip(r - NA_KH // 2, 0, rows - NA_KH)
        starts.append(pl.multiple_of(rs * gw, gw))
        half = NA_KH // 2
        entries.append(jnp.where(r < half, r, jnp.where(r > rows - half, r - (rows - half + 1) + half + 1, half)))
    pair = lambda dr: slice(2 * dr * gw, 2 * (dr + 1) * gw)
    s_ctx = _dot_nt(qq, kc_ref[...])
    s_loc = jnp.concatenate(
        [_dot_nt(qq[pair(dr)], k_ref[pl.ds(starts[dr], win), :]) + tab_ref[entries[dr]]
         for dr in range(NA_QROWS)], axis=0)
    m = jnp.maximum(jnp.max(s_loc, axis=-1, keepdims=True), jnp.max(s_ctx, axis=-1, keepdims=True))
    p_loc = jnp.exp(s_loc - m)
    p_ctx = jnp.exp(s_ctx - m)
    den = jnp.sum(p_loc, axis=-1, keepdims=True) + jnp.sum(p_ctx, axis=-1, keepdims=True)
    p_loc = p_loc.astype(_BF16)
    o = jnp.concatenate(
        [_dot(p_loc[pair(dr)], v_ref[pl.ds(starts[dr], win), :]) for dr in range(NA_QROWS)], axis=0)
    o = (o + _dot(p_ctx.astype(_BF16), vc_ref[...])) / den
    h0_lanes = lax.broadcasted_iota(jnp.int32, (gw, q.shape[1]), 1) < dh
    for dr in range(NA_QROWS):
        o_ref[dr * gw:(dr + 1) * gw, :] = jnp.where(
            h0_lanes, o[2 * dr * gw:(2 * dr + 1) * gw], o[(2 * dr + 1) * gw:(2 * dr + 2) * gw]).astype(_BF16)


def _na_latent(q, k, v, kc, vc, rpb):
    b, l, aw = q.shape
    lc = kc.shape[1]
    rows = l // GRID_W
    assert rows % NA_QROWS == 0 and rows >= 2 * NA_QROWS
    nb = rows // NA_QROWS
    hp = aw // (2 * NA_HEAD_DIM)
    tq = NA_QROWS * GRID_W
    row_entries = _na_row_entries(rows)
    tables = _na_column_tables(rpb.astype(_F32))
    tables = tables.reshape((hp, 2) + tables.shape[1:])
    return pl.pallas_call(
        functools.partial(_na_kernel, rows=rows, nb=nb, row_entries=row_entries),
        grid=(hp, b, nb),
        in_specs=[pl.BlockSpec((None, tq, 2 * NA_HEAD_DIM), lambda p, bi, j: (bi, j, p)),
                  pl.BlockSpec((None, l, 2 * NA_HEAD_DIM), lambda p, bi, j: (bi, 0, p)),
                  pl.BlockSpec((None, l, 2 * NA_HEAD_DIM), lambda p, bi, j: (bi, 0, p)),
                  pl.BlockSpec((None, lc, 2 * NA_HEAD_DIM), lambda p, bi, j: (bi, 0, p)),
                  pl.BlockSpec((None, lc, 2 * NA_HEAD_DIM), lambda p, bi, j: (bi, 0, p)),
                  pl.BlockSpec((None,) + tables.shape[1:], lambda p, bi, j: (p, 0, 0, 0, 0))],
        out_specs=pl.BlockSpec((None, tq, 2 * NA_HEAD_DIM), lambda p, bi, j: (bi, j, p)),
        out_shape=jax.ShapeDtypeStruct((b, l, aw), _BF16),
        scratch_shapes=[pltpu.VMEM((len(row_entries), 2 * GRID_W, NA_KH * GRID_W), _F32)],
        compiler_params=_cparams("arbitrary", "arbitrary", "arbitrary"),
        name="na_latent",
    )(q, k, v, kc, vc, tables)


def _ctx_attn_kernel(q_ref, k_ref, v_ref, o_ref):
    dh = NA_HEAD_DIM
    outs = []
    for h in range(2):
        ls = slice(h * dh, (h + 1) * dh)
        s = _dot_nt(q_ref[:, ls], k_ref[:, ls])
        m = jnp.max(s, axis=-1, keepdims=True)
        p = jnp.exp(s - m)
        den = jnp.sum(p, axis=-1, keepdims=True)
        outs.append(_dot(p.astype(_BF16), v_ref[:, ls]) / den)
    o_ref[...] = jnp.concatenate(outs, axis=-1).astype(_BF16)


def _ctx_attention(q, k, v):
    b, lc, aw = q.shape
    hp = aw // (2 * NA_HEAD_DIM)
    spec = pl.BlockSpec((None, lc, 2 * NA_HEAD_DIM), lambda bi, p: (bi, 0, p))
    return pl.pallas_call(
        _ctx_attn_kernel,
        grid=(b, hp),
        in_specs=[spec] * 3, out_specs=spec,
        out_shape=jax.ShapeDtypeStruct((b, lc, aw), _BF16),
        compiler_params=_cparams("parallel", "parallel"),
        name="ctx_attention",
    )(q, k, v)


def _sigmoid(z):
    return 0.5 * jnp.tanh(0.5 * z) + 0.5


def _gelu_tanh(x):
    return 0.5 * x * (1.0 + jnp.tanh(math.sqrt(2.0 / math.pi) * (x + 0.044715 * (x * x * x))))


def _merge_kernel(x_ref, mod_ref, h_ref, fa_ref, u_ref, yf_ref, yr_ref, nc_ref, d_ref, wglu_ref,
                  wga_ref, wgb_ref, wgc_ref, wa_ref, wb_ref, wc_ref, wo_ref, o_ref):
    x = x_ref[...]
    h = h_ref[...]
    y = u_ref[...].astype(_F32) * d_ref[...] + yf_ref[...].astype(_F32) + yr_ref[...].astype(_F32)
    gl = _gelu_tanh(y)
    sb = gl * _sigmoid(_dot(gl.astype(_BF16), wglu_ref[...]))
    m = _sigmoid(_dot(h, wga_ref[...])) * _dot(fa_ref[...], wa_ref[...])
    m += _sigmoid(_dot(h, wgb_ref[...])) * _dot(sb.astype(_BF16), wb_ref[...])
    m += _sigmoid(_dot(h, wgc_ref[...])) * _dot(nc_ref[...], wc_ref[...])
    o_ref[...] = x + mod_ref[2:3, :] * _dot(m.astype(_BF16), wo_ref[...])


def _merge(x, mod, h, fa, u, yf, yr, nc, d_skip, w_glu, w_in, w_a, w_b, w_c, w_out, layer, mod_row, off_g, tm):
    b, l, d = x.shape
    assert off_g % d == 0
    mod_spec, mat = _layer_specs(layer, mod_row, d)
    gate = lambda k: pl.BlockSpec((None, d, d), lambda bi, i: (layer, 0, off_g // d + k))
    tok = lambda a: pl.BlockSpec((None, tm, a.shape[2]), lambda bi, i: (bi, i, 0))
    tmaj = lambda a: pl.BlockSpec((tm, a.shape[1] // b), lambda bi, i: (i, bi))
    return pl.pallas_call(
        _merge_kernel,
        grid=(b, l // tm),
        in_specs=[tok(x), mod_spec, tok(h), tok(fa), tmaj(u), tmaj(yf), tmaj(yr), tok(nc), mat(d_skip), mat(w_glu),
                  gate(0), gate(1), gate(2), mat(w_a), mat(w_b), mat(w_c), mat(w_out)],
        out_specs=tok(x),
        out_shape=jax.ShapeDtypeStruct((b, l, d), _F32),
        compiler_params=_cparams("parallel", "parallel"),
        name="merge",
    )(x, mod, h, fa, u, yf, yr, nc, d_skip, w_glu, w_in, w_in, w_in, w_a, w_b, w_c, w_out)


def _ffn_kernel(x_ref, mod_ref, g_ref, w1_ref, w2_ref, gf_ref, o_ref, *, final_norm, fchunk):
    x = x_ref[...]
    h = _norm_mod(x, g_ref[...], mod_ref[3:4, :], mod_ref[4:5, :]).astype(_BF16)
    dff = w1_ref.shape[1]
    acc = jnp.zeros_like(x)
    for c in range(dff // fchunk):
        cs = slice(c * fchunk, (c + 1) * fchunk)
        t = jnp.maximum(_dot(h, w1_ref[:, cs]), 0.0)
        acc += _dot((t * t).astype(_BF16), w2_ref[cs, :])
    x2 = x + mod_ref[5:6, :] * acc
    if final_norm:
        ms = jnp.mean(x2 * x2, axis=-1, keepdims=True)
        x2 = x2 * lax.rsqrt(ms + RMS_EPS) * gf_ref[...]
    o_ref[...] = x2


def _ffn(x, mod, g, w1, w2, g_final, final_norm, layer, mod_row, tm):
    b, l, d = x.shape
    mod_spec, mat = _layer_specs(layer, mod_row, d)
    tok = pl.BlockSpec((None, tm, d), lambda bi, i: (bi, i, 0))
    full = lambda a: pl.BlockSpec(a.shape, lambda bi, i: (0, 0))
    return pl.pallas_call(
        functools.partial(_ffn_kernel, final_norm=final_norm, fchunk=FFN_CHUNK),
        grid=(b, l // tm),
        in_specs=[tok, mod_spec, mat(g), mat(w1), mat(w2), full(g_final)],
        out_specs=tok,
        out_shape=jax.ShapeDtypeStruct((b, l, d), _F32),
        compiler_params=_cparams("parallel", "parallel"),
        name="ffn",
    )(x, mod, g, w1, w2, g_final)


def kernel(x, c, ctx, c_ctx, w_mod, b_mod, g_norm1, g_norm2, w_in, w_br_a, w_br_b, w_br_c, w_out, s5_lam_re, s5_lam_im, s5_log_dt, s5_b_re, s5_b_im, s5_c_re, s5_c_im, s5_d, s5_w_glu, na_rpb, w_ff1, w_ff2, g_final):
    b, l, d = x.shape
    lc = ctx.shape[1]
    depth = w_mod.shape[0]
    fw = w_br_a.shape[1]
    sw = w_br_b.shape[1]
    aw = w_br_c.shape[1]
    off_g = fw + sw + 3 * aw
    tm = TOKEN_TILE
    tmc = lc

    c8 = jnp.concatenate([c, c_ctx[None, :], jnp.zeros((SUBLANES - b - 1, d), _F32)], axis=0)
    mod = _modulation(c8, w_mod, b_mod).reshape(depth, SUBLANES, 6, d)
    gf = g_final.reshape(1, d)
    g1 = g_norm1.reshape(depth, 1, d)
    g2 = g_norm2.reshape(depth, 1, d)
    d_skip = s5_d.reshape(depth, 1, sw)
    bf = lambda t: t.astype(_BF16)
    w_in_b, w_a, w_b, w_c, w_o, w_glu, w1, w2 = (bf(w_in), bf(w_br_a), bf(w_br_b), bf(w_br_c), bf(w_out),
                                                 bf(s5_w_glu), bf(w_ff1), bf(w_ff2))
    widths = (fw, sw, aw)

    for i in range(depth):
        need_ctx_out = i < depth - 1
        h_x, zf_x, zs_x, q_x, k_x, v_x = _inproj(x, mod, g1, w_in_b, i, None, widths, tm)
        h_c, zf_c, zs_c, q_c, k_c, v_c = _inproj(ctx, mod, g1, w_in_b, i, b, widths, tmc)

        fa_x = _fnet_latent(zf_x)
        (yf_x, yr_x), (yf_c, yr_c) = _s5_mix(zs_x, zs_c, b, s5_lam_re[i], s5_lam_im[i], s5_log_dt[i],
                                              s5_b_re[i], s5_b_im[i], s5_c_re[i], s5_c_im[i])
        nc_x = _na_latent(q_x, k_x, v_x, k_c, v_c, na_rpb[i])

        branch_w = (d_skip, w_glu, w_in_b, w_a, w_b, w_c, w_o)
        x = _merge(x, mod, h_x, fa_x, zs_x, yf_x, yr_x, nc_x, *branch_w, i, None, off_g, tm)
        x = _ffn(x, mod, g2, w1, w2, gf, not need_ctx_out, i, None, tm)

        if need_ctx_out:
            fa_c = _fnet_ctx(zf_c)
            nc_c = _ctx_attention(q_c, k_c, v_c)
            ctx = _merge(ctx, mod, h_c, fa_c, zs_c, yf_c, yr_c, nc_c, *branch_w, i, b, off_g, tmc)
            ctx = _ffn(ctx, mod, g2, w1, w2, gf, False, i, b, tmc)
    return x
```
